```python
import jax, jax.numpy as jnp
from jax import lax
import numpy as np


D_MODEL = 1024
BATCH = 16
SEQ = 4096
DEPTH = 4
DEC_BATCH = 32
DEC_SEQ = 16
PAST_LEN = 1024

CHUNK = 64
QBLOCK = 128
PLE_DIM = 256
H_RET = 8
DK_RET = D_MODEL // H_RET
DV_RET = 2 * DK_RET
RET_QK = H_RET * DK_RET
RET_V = H_RET * DV_RET
RET_IN = 2 * RET_QK + 2 * RET_V
H_SB = 16
HD_SB = D_MODEL // H_SB
D_FF = 2816
N_RET_LAYERS = (DEPTH + 1) // 2
N_SB_LAYERS = DEPTH // 2
ROPE_BASE = 10000.0
EPS = 1e-6
GN_EPS = 1e-5

kernel_name = 'retnet_stickbreaking_macaron_stream_step'

F32 = jnp.float32


def rmsnorm(x, g):
    xf = x.astype(F32)
    y = xf * lax.rsqrt(jnp.mean(xf * xf, axis=-1, keepdims=True) + EPS)
    return (y * g.astype(F32)).astype(x.dtype)


def half_swiglu(x, g, w_gu, w_down):
    h = rmsnorm(x, g) @ w_gu
    a, b = jnp.split(h, 2, axis=-1)
    return x + 0.5 * ((jax.nn.silu(a) * b) @ w_down)


def ple_add(x, p_i, g, w_gate, w_proj):
    gate = jax.nn.sigmoid((rmsnorm(x, g) @ w_gate).astype(F32))
    return x + (gate * (p_i @ w_proj).astype(F32)).astype(x.dtype)


def rotary(x, pos):
    half = x.shape[-1] // 2
    freqs = ROPE_BASE ** (-jnp.arange(half, dtype=F32) / half)
    ang = pos.astype(F32)[:, None] * freqs[None, :]
    cos = jnp.cos(ang)[None, :, None, :]
    sin = jnp.sin(ang)[None, :, None, :]
    xf = x.astype(F32)
    x1, x2 = xf[..., :half], xf[..., half:]
    return jnp.concatenate([x1 * cos - x2 * sin, x1 * sin + x2 * cos], axis=-1)


def retention_scan(q, k, v, s0):
    B, L, H, DK = q.shape
    DV = v.shape[-1]
    C = min(CHUNK, L)
    nc = L // C
    log_g = jnp.log1p(-(2.0 ** (-5.0 - jnp.arange(H, dtype=F32))))
    idx = jnp.arange(C, dtype=F32)
    rel = idx[:, None] - idx[None, :]
    decay_in = jnp.where(rel[None] >= 0, jnp.exp(jnp.maximum(rel, 0.0)[None] * log_g[:, None, None]), 0.0)
    decay_q = jnp.exp((idx + 1.0)[:, None] * log_g[None, :])
    decay_k = jnp.exp((C - 1.0 - idx)[:, None] * log_g[None, :])
    decay_c = jnp.exp(C * log_g)

    def to_chunks(t):
        return t.reshape(B, nc, C, H, t.shape[-1]).swapaxes(0, 1)

    def step(s, inp):
        qi, ki, vi = inp
        scores = jnp.einsum('bnhd,bmhd->bhnm', qi, ki) * decay_in[None]
        o = (jnp.einsum('bhnm,bmhe->bnhe', scores, vi)
             + jnp.einsum('bnhd,bhde->bnhe', qi, s) * decay_q[None, :, :, None])
        s = (decay_c[None, :, None, None] * s
             + jnp.einsum('bmhd,bmhe->bhde', ki * decay_k[None, :, :, None], vi))
        return s, o

    s_final, o = lax.scan(step, s0.astype(F32), (to_chunks(q), to_chunks(k), to_chunks(v)))
    return o.swapaxes(0, 1).reshape(B, L, H, DV), s_final


def retention_mixer(xn, pos, s0, w_in, w_out):
    B, L, _ = xn.shape
    proj = xn @ w_in
    q = proj[..., :RET_QK].reshape(B, L, H_RET, DK_RET)
    k = proj[..., RET_QK:2 * RET_QK].reshape(B, L, H_RET, DK_RET)
    v = proj[..., 2 * RET_QK:2 * RET_QK + RET_V].reshape(B, L, H_RET, DV_RET).astype(F32)
    g = proj[..., 2 * RET_QK + RET_V:].astype(F32)
    q = rotary(q, pos)
    k = rotary(k, pos) * (DK_RET ** -0.5)
    o, s_new = retention_scan(q, k, v, s0)
    mu = jnp.mean(o, axis=-1, keepdims=True)
    var = jnp.mean(jnp.square(o - mu), axis=-1, keepdims=True)
    o = (o - mu) * lax.rsqrt(var + GN_EPS)
    y = (jax.nn.silu(g) * o.reshape(B, L, RET_V)).astype(xn.dtype) @ w_out
    return y, s_new


def sb_project(xn, w_qkv):
    B, L, _ = xn.shape
    qkv = (xn @ w_qkv).reshape(B, L, 3, H_SB, HD_SB)
    return qkv[:, :, 0], qkv[:, :, 1], qkv[:, :, 2]


def sb_block(q, k, v, q_pos, k_pos):
    z = jnp.einsum('bqhd,bkhd->bhqk', q.astype(F32), k.astype(F32)) * (HD_SB ** -0.5)
    valid = (k_pos[None, :] < q_pos[:, None])[None, None]
    log_1m = jnp.where(valid, jax.nn.log_sigmoid(-z), 0.0)
    log_a = jax.nn.log_sigmoid(z) + lax.cumsum(log_1m, axis=3, reverse=True) - log_1m
    a = jnp.where(valid, jnp.exp(log_a), 0.0)
    return jnp.einsum('bhqk,bkhd->bqhd', a, v.astype(F32)).astype(q.dtype)


def sb_prompt(q, k, v):
    B, S, H, D = q.shape
    nb = S // QBLOCK
    pos = jnp.arange(S)
    qb = q.reshape(B, nb, QBLOCK, H, D).swapaxes(0, 1)
    pb = pos.reshape(nb, QBLOCK)
    ob = lax.map(lambda args: sb_block(args[0], k, v, args[1], pos), (qb, pb))
    return ob.swapaxes(0, 1).reshape(B, S, H, D)


def setup_inputs(seed: int = 0) -> dict:
    key = jax.random.key(seed)
    ks = jax.random.split(key, 24)

    def nrm(k, shape, scale):
        return jax.random.normal(k, shape, F32) * scale

    def gain(k, shape):
        return 1.0 + 0.02 * jax.random.normal(k, shape, F32)

    return {
        'x_prompt': nrm(ks[0], (BATCH, SEQ, D_MODEL), 1.0),
        'x_sample': nrm(ks[1], (DEC_BATCH, DEC_SEQ, D_MODEL), 1.0),
        'p_prompt': nrm(ks[2], (DEPTH, BATCH, SEQ, PLE_DIM), 1.0),
        'p_sample': nrm(ks[3], (DEPTH, DEC_BATCH, DEC_SEQ, PLE_DIM), 1.0),
        'state_ret': nrm(ks[4], (N_RET_LAYERS, DEC_BATCH, H_RET, DK_RET, DV_RET), 0.5),
        'cache_sb_k': nrm(ks[5], (N_SB_LAYERS, DEC_BATCH, PAST_LEN, H_SB, HD_SB), 1.0),
        'cache_sb_v': nrm(ks[6], (N_SB_LAYERS, DEC_BATCH, PAST_LEN, H_SB, HD_SB), 1.0),
        'norm_ffn_a': gain(ks[7], (DEPTH, D_MODEL)),
        'w_ffn_a_gu': nrm(ks[8], (DEPTH, D_MODEL, 2 * D_FF), D_MODEL ** -0.5),
        'w_ffn_a_down': nrm(ks[9], (DEPTH, D_FF, D_MODEL), D_FF ** -0.5),
        'norm_mix': gain(ks[10], (DEPTH, D_MODEL)),
        'ret_w_in': nrm(ks[11], (N_RET_LAYERS, D_MODEL, RET_IN), D_MODEL ** -0.5),
        'ret_w_out': nrm(ks[12], (N_RET_LAYERS, RET_V, D_MODEL), RET_V ** -0.5),
        'sb_w_qkv': nrm(ks[13], (N_SB_LAYERS, D_MODEL, 3 * H_SB * HD_SB), D_MODEL ** -0.5),
        'sb_w_out': nrm(ks[14], (N_SB_LAYERS, H_SB * HD_SB, D_MODEL), (H_SB * HD_SB) ** -0.5),
        'norm_ffn_b': gain(ks[15], (DEPTH, D_MODEL)),
        'w_ffn_b_gu': nrm(ks[16], (DEPTH, D_MODEL, 2 * D_FF), D_MODEL ** -0.5),
        'w_ffn_b_down': nrm(ks[17], (DEPTH, D_FF, D_MODEL), D_FF ** -0.5),
        'norm_ple': gain(ks[18], (DEPTH, D_MODEL)),
        'ple_w_gate': nrm(ks[19], (DEPTH, D_MODEL, D_MODEL), D_MODEL ** -0.5),
        'ple_w_proj': nrm(ks[20], (DEPTH, PLE_DIM, D_MODEL), PLE_DIM ** -0.5),
        'norm_final': gain(ks[21], (D_MODEL,)),
    }


def reference(x_prompt, x_sample, p_prompt, p_sample, state_ret, cache_sb_k, cache_sb_v,
              norm_ffn_a, w_ffn_a_gu, w_ffn_a_down, norm_mix, ret_w_in, ret_w_out,
              sb_w_qkv, sb_w_out, norm_ffn_b, w_ffn_b_gu, w_ffn_b_down,
              norm_ple, ple_w_gate, ple_w_proj, norm_final):
    Bp, Sp, _ = x_prompt.shape
    Ls = x_sample.shape[1]
    past = cache_sb_k.shape[2]
    pos_p = jnp.arange(Sp)
    pos_s = past + jnp.arange(Ls)
    k_pos_s = jnp.arange(past + Ls)

    hp, hs = x_prompt, x_sample
    ret_p, ret_s, kp_l, vp_l, ks_l, vs_l = [], [], [], [], [], []
    for i in range(DEPTH):
        j = i // 2
        hp = half_swiglu(hp, norm_ffn_a[i], w_ffn_a_gu[i], w_ffn_a_down[i])
        hs = half_swiglu(hs, norm_ffn_a[i], w_ffn_a_gu[i], w_ffn_a_down[i])
        xn_p = rmsnorm(hp, norm_mix[i])
        xn_s = rmsnorm(hs, norm_mix[i])
        if i % 2 == 0:
            s0 = jnp.zeros((Bp, H_RET, DK_RET, DV_RET), F32)
            yp, sp = retention_mixer(xn_p, pos_p, s0, ret_w_in[j], ret_w_out[j])
            ys, ss = retention_mixer(xn_s, pos_s, state_ret[j], ret_w_in[j], ret_w_out[j])
            ret_p.append(sp)
            ret_s.append(ss)
        else:
            qp, kp, vp = sb_project(xn_p, sb_w_qkv[j])
            op = sb_prompt(qp, kp, vp)
            yp = op.reshape(Bp, Sp, H_SB * HD_SB) @ sb_w_out[j]
            qs, kn, vn = sb_project(xn_s, sb_w_qkv[j])
            k_all = jnp.concatenate([cache_sb_k[j].astype(kn.dtype), kn], axis=1)
            v_all = jnp.concatenate([cache_sb_v[j].astype(vn.dtype), vn], axis=1)
            os_ = sb_block(qs, k_all, v_all, pos_s, k_pos_s)
            ys = os_.reshape(os_.shape[0], Ls, H_SB * HD_SB) @ sb_w_out[j]
            kp_l.append(kp)
            vp_l.append(vp)
            ks_l.append(kn)
            vs_l.append(vn)
        hp = hp + yp.astype(hp.dtype)
        hs = hs + ys.astype(hs.dtype)
        hp = half_swiglu(hp, norm_ffn_b[i], w_ffn_b_gu[i], w_ffn_b_down[i])
        hs = half_swiglu(hs, norm_ffn_b[i], w_ffn_b_gu[i], w_ffn_b_down[i])
        hp = ple_add(hp, p_prompt[i], norm_ple[i], ple_w_gate[i], ple_w_proj[i])
        hs = ple_add(hs, p_sample[i], norm_ple[i], ple_w_gate[i], ple_w_proj[i])

    y_prompt = rmsnorm(hp, norm_final)
    y_sample = rmsnorm(hs, norm_final)
    state_ret_prompt = jnp.stack(ret_p)
    state_ret_sample = jnp.stack(ret_s)
    sb_k_prompt = jnp.stack(kp_l)
    sb_v_prompt = jnp.stack(vp_l)
    sb_k_sample = jnp.stack(ks_l)
    sb_v_sample = jnp.stack(vs_l)
    return (y_prompt, y_sample, state_ret_prompt, state_ret_sample,
            sb_k_prompt, sb_v_prompt, sb_k_sample, sb_v_sample)
```

```python
import functools

import jax
import jax.numpy as jnp
from jax import lax
from jax.experimental import pallas as pl
from jax.experimental.pallas import tpu as pltpu

F32 = jnp.float32
BF16 = jnp.bfloat16

H_RET = 8
H_SB = 16
RET_CHUNK = 64
ROPE_BASE = 10000.0
EPS = 1e-6
GN_EPS = 1e-5

LANES = 128
VMEM_LIMIT_BYTES = 56 * 1024 * 1024
TOKEN_TILE = 512
PROJ_TOKEN_TILE = 256
FFN_CHUNKS = 2
RET_BLOCK = 256
SB_BLOCK = 256


def _params(*semantics):
    return pltpu.CompilerParams(dimension_semantics=semantics, vmem_limit_bytes=VMEM_LIMIT_BYTES)


def _rmsnorm(x, g):
    return x * lax.rsqrt(jnp.mean(x * x, axis=-1, keepdims=True) + EPS) * g


def _dot(a, b):
    return jnp.dot(a, b, preferred_element_type=F32)


def _dot_nt(a, b):
    return lax.dot_general(a, b, (((1,), (1,)), ((), ())), preferred_element_type=F32)


def _dot_tn(a, b):
    return lax.dot_general(a, b, (((0,), (0,)), ((), ())), preferred_element_type=F32)


def _ffn_kernel(x_ref, g_ref, wa_ref, wb_ref, wd_ref, o_ref, xn_ref, acc_ref):
    f = pl.program_id(1)

    @pl.when(f == 0)
    def _():
        xn_ref[...] = _rmsnorm(x_ref[...], g_ref[...]).astype(BF16)
        acc_ref[...] = jnp.zeros_like(acc_ref)

    xn = xn_ref[...]
    a = _dot(xn, wa_ref[...])
    b = _dot(xn, wb_ref[...])
    h = (a * jax.nn.sigmoid(a) * b).astype(BF16)
    acc_ref[...] += _dot(h, wd_ref[...])

    @pl.when(f == pl.num_programs(1) - 1)
    def _():
        o_ref[...] = x_ref[...] + 0.5 * acc_ref[...]


def _ffn(x, g, w_gu, w_down):
    t, d = x.shape
    d_ff = w_down.shape[0]
    tm = min(TOKEN_TILE, t)
    nf = FFN_CHUNKS
    tf = d_ff // nf
    assert t % tm == 0 and tf * nf == d_ff and tf % LANES == 0
    return pl.pallas_call(
        _ffn_kernel,
        grid=(t // tm, nf),
        in_specs=[
            pl.BlockSpec((tm, d), lambda i, f: (i, 0)),
            pl.BlockSpec((1, d), lambda i, f: (0, 0)),
            pl.BlockSpec((d, tf), lambda i, f: (0, f)),
            pl.BlockSpec((d, tf), lambda i, f: (0, f + nf)),
            pl.BlockSpec((tf, d), lambda i, f: (f, 0)),
        ],
        out_specs=pl.BlockSpec((tm, d), lambda i, f: (i, 0)),
        out_shape=jax.ShapeDtypeStruct((t, d), F32),
        scratch_shapes=[pltpu.VMEM((tm, d), BF16), pltpu.VMEM((tm, d), F32)],
        compiler_params=_params("parallel", "arbitrary"),
        name="ffn",
    )(x, g, w_gu, w_gu, w_down)


def _ple_kernel(x_ref, p_ref, g_ref, wg_ref, wp_ref, gf_ref, o_ref, *, final):
    x = x_ref[...]
    xn = _rmsnorm(x, g_ref[...]).astype(BF16)
    gate = jax.nn.sigmoid(_dot(xn, wg_ref[...]))
    y = x + gate * _dot(p_ref[...].astype(BF16), wp_ref[...])
    if final:
        y = _rmsnorm(y, gf_ref[...])
    o_ref[...] = y


def _ple(x, p, g, w_gate, w_proj, g_final, final):
    t, d = x.shape
    pd = p.shape[1]
    tm = min(TOKEN_TILE, t)
    assert t % tm == 0
    return pl.pallas_call(
        functools.partial(_ple_kernel, final=final),
        grid=(t // tm,),
        in_specs=[
            pl.BlockSpec((tm, d), lambda i: (i, 0)),
            pl.BlockSpec((tm, pd), lambda i: (i, 0)),
            pl.BlockSpec((1, d), lambda i: (0, 0)),
            pl.BlockSpec((d, d), lambda i: (0, 0)),
            pl.BlockSpec((pd, d), lambda i: (0, 0)),
            pl.BlockSpec((1, d), lambda i: (0, 0)),
        ],
        out_specs=pl.BlockSpec((tm, d), lambda i: (i, 0)),
        out_shape=jax.ShapeDtypeStruct((t, d), F32),
        compiler_params=_params("parallel"),
        name="ple",
    )(x, p, g, w_gate, w_proj, g_final)


def _outproj_kernel(a_ref, w_ref, x_ref, o_ref):
    o_ref[...] = x_ref[...] + _dot(a_ref[...], w_ref[...])


def _outproj(a, w, x):
    t, d = x.shape
    kdim = a.shape[1]
    tm = min(TOKEN_TILE, t)
    assert t % tm == 0
    return pl.pallas_call(
        _outproj_kernel,
        grid=(t // tm,),
        in_specs=[
            pl.BlockSpec((tm, kdim), lambda i: (i, 0)),
            pl.BlockSpec((kdim, d), lambda i: (0, 0)),
            pl.BlockSpec((tm, d), lambda i: (i, 0)),
        ],
        out_specs=pl.BlockSpec((tm, d), lambda i: (i, 0)),
        out_shape=jax.ShapeDtypeStruct((t, d), F32),
        compiler_params=_params("parallel"),
        name="outproj",
    )(a, w, x)


def _ret_proj_kernel(x_ref, g_ref, w_ref, cos_ref, sin_ref, q_ref, k_ref, v_ref, gate_ref):
    d = x_ref.shape[1]
    xn = _rmsnorm(x_ref[...], g_ref[...]).astype(BF16)
    cos = cos_ref[...]
    sin = sin_ref[...]
    for part, out in ((0, q_ref), (1, k_ref)):
        r = _dot(xn, w_ref[:, part * d:(part + 1) * d])
        for h in range(d // LANES):
            rh = r[:, h * LANES:(h + 1) * LANES]
            out[:, h * LANES:(h + 1) * LANES] = (rh * cos + pltpu.roll(rh, LANES // 2, 1) * sin).astype(BF16)
    for c in range(2):
        v_ref[:, c * d:(c + 1) * d] = _dot(xn, w_ref[:, (2 + c) * d:(3 + c) * d]).astype(BF16)
        gate_ref[:, c * d:(c + 1) * d] = _dot(xn, w_ref[:, (4 + c) * d:(5 + c) * d])


def _ret_proj(x, g, w_in, cos, sin):
    t, d = x.shape
    tm = min(PROJ_TOKEN_TILE, t)
    nrot = cos.shape[0] // tm
    assert t % tm == 0 and cos.shape[0] % tm == 0 and w_in.shape[1] == 6 * d
    return pl.pallas_call(
        _ret_proj_kernel,
        grid=(t // tm,),
        in_specs=[
            pl.BlockSpec((tm, d), lambda i: (i, 0)),
            pl.BlockSpec((1, d), lambda i: (0, 0)),
            pl.BlockSpec((d, 6 * d), lambda i: (0, 0)),
            pl.BlockSpec((tm, LANES), lambda i: (i % nrot, 0)),
            pl.BlockSpec((tm, LANES), lambda i: (i % nrot, 0)),
        ],
        out_specs=[
            pl.BlockSpec((tm, d), lambda i: (i, 0)),
            pl.BlockSpec((tm, d), lambda i: (i, 0)),
            pl.BlockSpec((tm, 2 * d), lambda i: (i, 0)),
            pl.BlockSpec((tm, 2 * d), lambda i: (i, 0)),
        ],
        out_shape=[
            jax.ShapeDtypeStruct((t, d), BF16),
            jax.ShapeDtypeStruct((t, d), BF16),
            jax.ShapeDtypeStruct((t, 2 * d), BF16),
            jax.ShapeDtypeStruct((t, 2 * d), F32),
        ],
        compiler_params=_params("parallel"),
        name="ret_proj",
    )(x, g, w_in, cos, sin)


def _ret_scan_kernel(q_ref, k_ref, v_ref, g_ref, s0_ref, din_ref, dq_ref, dk_ref, dc_ref,
                     y_ref, so_ref, s_ref):
    c = pl.program_id(1)
    dk_w = q_ref.shape[1] // H_RET
    dv_w = v_ref.shape[1] // H_RET

    @pl.when(c == 0)
    def _():
        s_ref[...] = s0_ref[0]

    for h in range(H_RET):
        q = q_ref[:, h * dk_w:(h + 1) * dk_w]
        k = k_ref[:, h * dk_w:(h + 1) * dk_w]
        v = v_ref[:, h * dv_w:(h + 1) * dv_w]
        s = s_ref[h]
        scores = _dot_nt(q, k) * din_ref[h]
        o = _dot(scores.astype(BF16), v) + _dot(q, s.astype(BF16)) * dq_ref[h]
        k_dec = (k.astype(F32) * dk_ref[h]).astype(BF16)
        s_ref[h] = dc_ref[h] * s + _dot_tn(k_dec, v)
        mu = jnp.mean(o, axis=-1, keepdims=True)
        oc = o - mu
        var = jnp.mean(oc * oc, axis=-1, keepdims=True)
        on = oc * lax.rsqrt(var + GN_EPS)
        gt = g_ref[:, h * dv_w:(h + 1) * dv_w]
        y_ref[:, h * dv_w:(h + 1) * dv_w] = (gt * jax.nn.sigmoid(gt) * on).astype(BF16)

    @pl.when(c == pl.num_programs(1) - 1)
    def _():
        so_ref[0] = s_ref[...]


def _ret_tables(c, dk_w, dv_w):
    scale = dk_w ** -0.5
    log_g = jnp.log1p(-(2.0 ** (-5.0 - jnp.arange(H_RET, dtype=F32))))
    idx = jnp.arange(c, dtype=F32)
    rel = idx[:, None] - idx[None, :]
    din = jnp.where(rel[None] >= 0, jnp.exp(jnp.maximum(rel, 0.0)[None] * log_g[:, None, None]), 0.0) * scale
    dq = jnp.exp((idx + 1.0)[None, :] * log_g[:, None])
    dk = jnp.exp((c - 1.0 - idx)[None, :] * log_g[:, None]) * scale
    dc = jnp.exp(c * log_g)
    dq = jnp.broadcast_to(dq[:, :, None], (H_RET, c, dv_w))
    dk = jnp.broadcast_to(dk[:, :, None], (H_RET, c, dk_w))
    dc = jnp.broadcast_to(dc[:, None, None], (H_RET, 1, dv_w))
    return din, dq, dk, dc


def _ret_scan(q, k, v, gate, s0, seq_len, chunk):
    t, dqk = q.shape
    dvv = v.shape[1]
    b = t // seq_len
    nc = seq_len // chunk
    dk_w, dv_w = dqk // H_RET, dvv // H_RET
    assert nc * chunk == seq_len and s0.shape == (b, H_RET, dk_w, dv_w)
    din, dq, dk, dc = _ret_tables(chunk, dk_w, dv_w)
    tok = lambda bi, ci: (bi * nc + ci, 0)
    full3 = lambda bi, ci: (0, 0, 0)
    state = lambda bi, ci: (bi, 0, 0, 0)
    return pl.pallas_call(
        _ret_scan_kernel,
        grid=(b, nc),
        in_specs=[
            pl.BlockSpec((chunk, dqk), tok),
            pl.BlockSpec((chunk, dqk), tok),
            pl.BlockSpec((chunk, dvv), tok),
            pl.BlockSpec((chunk, dvv), tok),
            pl.BlockSpec((1, H_RET, dk_w, dv_w), state),
            pl.BlockSpec((H_RET, chunk, chunk), full3),
            pl.BlockSpec((H_RET, chunk, dv_w), full3),
            pl.BlockSpec((H_RET, chunk, dk_w), full3),
            pl.BlockSpec((H_RET, 1, dv_w), full3),
        ],
        out_specs=[
            pl.BlockSpec((chunk, dvv), tok),
            pl.BlockSpec((1, H_RET, dk_w, dv_w), state),
        ],
        out_shape=[
            jax.ShapeDtypeStruct((t, dvv), BF16),
            jax.ShapeDtypeStruct((b, H_RET, dk_w, dv_w), F32),
        ],
        scratch_shapes=[pltpu.VMEM((H_RET, dk_w, dv_w), F32)],
        compiler_params=_params("parallel", "arbitrary"),
        name="ret_scan",
    )(q, k, v, gate, s0, din, dq, dk, dc)


def _sb_proj_kernel(x_ref, g_ref, w_ref, q_ref, kf_ref, vf_ref, kb_ref, vb_ref, *, scale):
    d = x_ref.shape[1]
    xn = _rmsnorm(x_ref[...], g_ref[...]).astype(BF16)
    q_ref[...] = (_dot(xn, w_ref[:, 0:d]) * scale).astype(BF16)
    k = _dot(xn, w_ref[:, d:2 * d])
    kf_ref[...] = k
    kb_ref[...] = k.astype(BF16)
    v = _dot(xn, w_ref[:, 2 * d:3 * d])
    vf_ref[...] = v
    vb_ref[...] = v.astype(BF16)


def _sb_proj(x, g, w_qkv):
    t, d = x.shape
    tm = min(TOKEN_TILE, t)
    assert t % tm == 0 and w_qkv.shape[1] == 3 * d
    scale = (d // H_SB) ** -0.5
    tok = pl.BlockSpec((tm, d), lambda i: (i, 0))
    return pl.pallas_call(
        functools.partial(_sb_proj_kernel, scale=scale),
        grid=(t // tm,),
        in_specs=[
            tok,
            pl.BlockSpec((1, d), lambda i: (0, 0)),
            pl.BlockSpec((d, 3 * d), lambda i: (0, 0)),
        ],
        out_specs=[tok, tok, tok, tok, tok],
        out_shape=[
            jax.ShapeDtypeStruct((t, d), BF16),
            jax.ShapeDtypeStruct((t, d), F32),
            jax.ShapeDtypeStruct((t, d), F32),
            jax.ShapeDtypeStruct((t, d), BF16),
            jax.ShapeDtypeStruct((t, d), BF16),
        ],
        compiler_params=_params("parallel"),
        name="sb_proj",
    )(x, g, w_qkv)


def _sb_block(q, k, v, tri, carry, acc, mask):
    z = _dot_nt(q, k)
    sp = jnp.maximum(z, 0.0) + jnp.log(1.0 + jnp.exp(-jnp.abs(z)))
    if mask is not None:
        sp = jnp.where(mask, sp, 0.0)
    hi = sp.astype(BF16)
    lo = (sp - hi.astype(F32)).astype(BF16)
    newer = _dot(hi, tri) + _dot(lo, tri)
    a = jnp.exp(z - sp - newer - carry)
    if mask is not None:
        a = jnp.where(mask, a, 0.0)
    acc = acc + _dot(a.astype(BF16), v)
    carry = carry + jnp.sum(sp, axis=-1, keepdims=True)
    return carry, acc


def _strict_tri(n):
    r = lax.broadcasted_iota(jnp.int32, (n, n), 0)
    c = lax.broadcasted_iota(jnp.int32, (n, n), 1)
    return r, c


def _sb_attn_kernel(q_ref, k_ref, v_ref, tri_ref, o_ref):
    qi = pl.program_id(2)
    tq = q_ref.shape[0]
    hd = q_ref.shape[1] // 2
    tri = tri_ref[...]
    row, col = _strict_tri(tq)
    diag_mask = col < row
    for hh in range(2):
        lanes = slice(hh * hd, (hh + 1) * hd)
        q = q_ref[:, lanes]

        def step(kb, carry, acc, mask):
            ks = pl.multiple_of(kb * tq, tq)
            return _sb_block(q, k_ref[pl.ds(ks, tq), lanes], v_ref[pl.ds(ks, tq), lanes], tri, carry, acc, mask)

        carry = jnp.zeros((tq, 1), F32)
        acc = jnp.zeros((tq, hd), F32)
        carry, acc = step(qi, carry, acc, diag_mask)
        carry, acc = lax.fori_loop(0, qi, lambda t, ca: step(qi - 1 - t, ca[0], ca[1], None), (carry, acc))
        o_ref[:, lanes] = acc.astype(BF16)


def _tri_matrix(n):
    r, c = _strict_tri(n)
    return (r > c).astype(BF16)


def _sb_attn(q, k, v, seq_len):
    t, d = q.shape
    b = t // seq_len
    tq = min(SB_BLOCK, seq_len)
    nq = seq_len // tq
    hp = 2 * (d // H_SB)
    assert nq * tq == seq_len and hp == LANES
    qmap = lambda bi, hi, qi: (bi * nq + qi, hi)
    kvmap = lambda bi, hi, qi: (bi, hi)
    return pl.pallas_call(
        _sb_attn_kernel,
        grid=(b, d // hp, nq),
        in_specs=[
            pl.BlockSpec((tq, hp), qmap),
            pl.BlockSpec((seq_len, hp), kvmap),
            pl.BlockSpec((seq_len, hp), kvmap),
            pl.BlockSpec((tq, tq), lambda bi, hi, qi: (0, 0)),
        ],
        out_specs=pl.BlockSpec((tq, hp), qmap),
        out_shape=jax.ShapeDtypeStruct((t, d), BF16),
        compiler_params=_params("parallel", "parallel", "arbitrary"),
        name="sb_attn",
    )(q, k, v, _tri_matrix(tq))


def _sb_attn_sample_kernel(q_ref, kc_ref, vc_ref, kn_ref, vn_ref, tri_ref, trin_ref, o_ref, *, tk):
    ls = q_ref.shape[0]
    hd = q_ref.shape[1] // 2
    past = kc_ref.shape[0]
    tri = tri_ref[...]
    row, col = _strict_tri(ls)
    new_mask = col < row
    for hh in range(2):
        lanes = slice(hh * hd, (hh + 1) * hd)
        q = q_ref[:, lanes]
        carry = jnp.zeros((ls, 1), F32)
        acc = jnp.zeros((ls, hd), F32)
        carry, acc = _sb_block(q, kn_ref[:, lanes], vn_ref[:, lanes], trin_ref[...], carry, acc, new_mask)
        for kb in reversed(range(past // tk)):
            rows = slice(kb * tk, (kb + 1) * tk)
            carry, acc = _sb_block(q, kc_ref[rows, lanes].astype(BF16), vc_ref[rows, lanes].astype(BF16),
                                   tri, carry, acc, None)
        o_ref[:, lanes] = acc.astype(BF16)


def _sb_attn_sample(q, k_cache, v_cache, k_new, v_new, ls):
    t, d = q.shape
    b = t // ls
    past = k_cache.shape[0] // b
    tk = min(SB_BLOCK, past)
    hp = 2 * (d // H_SB)
    assert past % tk == 0 and hp == LANES
    new = pl.BlockSpec((ls, hp), lambda bi, hi: (bi, hi))
    old = pl.BlockSpec((past, hp), lambda bi, hi: (bi, hi))
    return pl.pallas_call(
        functools.partial(_sb_attn_sample_kernel, tk=tk),
        grid=(b, d // hp),
        in_specs=[
            new, old, old, new, new,
            pl.BlockSpec((tk, tk), lambda bi, hi: (0, 0)),
            pl.BlockSpec((ls, ls), lambda bi, hi: (0, 0)),
        ],
        out_specs=new,
        out_shape=jax.ShapeDtypeStruct((t, d), BF16),
        compiler_params=_params("parallel", "parallel"),
        name="sb_attn_sample",
    )(q, k_cache, v_cache, k_new, v_new, _tri_matrix(tk), _tri_matrix(ls))


def _rotary_tables(pos, half, reps):
    freqs = ROPE_BASE ** (-jnp.arange(half, dtype=F32) / half)
    ang = pos.astype(F32)[:, None] * freqs[None, :]
    cos, sin = jnp.cos(ang), jnp.sin(ang)
    cos = jnp.concatenate([cos, cos], axis=-1)
    sin = jnp.concatenate([-sin, sin], axis=-1)
    return jnp.tile(cos, (reps, 1)), jnp.tile(sin, (reps, 1))


def kernel(x_prompt, x_sample, p_prompt, p_sample, state_ret, cache_sb_k, cache_sb_v, norm_ffn_a, w_ffn_a_gu, w_ffn_a_down, norm_mix, ret_w_in, ret_w_out, sb_w_qkv, sb_w_out, norm_ffn_b, w_ffn_b_gu, w_ffn_b_down, norm_ple, ple_w_gate, ple_w_proj, norm_final):
    bp, sp, d = x_prompt.shape
    bs, ls, _ = x_sample.shape
    depth = norm_ffn_a.shape[0]
    past = cache_sb_k.shape[2]
    dk_w = d // H_RET
    dv_w = ret_w_out.shape[1] // H_RET
    hd = d // H_SB

    hp = x_prompt.reshape(bp * sp, d)
    hs = x_sample.reshape(bs * ls, d)
    pp = p_prompt.reshape(depth, bp * sp, -1)
    ps = p_sample.reshape(depth, bs * ls, -1)
    row = lambda g: g.reshape(1, d)

    cos_p, sin_p = _rotary_tables(jnp.arange(sp), dk_w // 2, 1)
    cos_s, sin_s = _rotary_tables(past + jnp.arange(ls), dk_w // 2, bs)
    ret_chunk_p = min(RET_BLOCK, sp)
    ret_chunk_s = min(RET_CHUNK, ls)
    zero_state = jnp.zeros((bp, H_RET, dk_w, dv_w), F32)
    g_final = row(norm_final)

    ret_p, ret_s, kp_l, vp_l, ks_l, vs_l = [], [], [], [], [], []
    for i in range(depth):
        j = i // 2
        wa_gu, wa_dn = w_ffn_a_gu[i].astype(BF16), w_ffn_a_down[i].astype(BF16)
        hp = _ffn(hp, row(norm_ffn_a[i]), wa_gu, wa_dn)
        hs = _ffn(hs, row(norm_ffn_a[i]), wa_gu, wa_dn)
        g_mix = row(norm_mix[i])
        if i % 2 == 0:
            w_in, w_out = ret_w_in[j].astype(BF16), ret_w_out[j].astype(BF16)
            q, k, v, gate = _ret_proj(hp, g_mix, w_in, cos_p, sin_p)
            y, s_new = _ret_scan(q, k, v, gate, zero_state, sp, ret_chunk_p)
            hp = _outproj(y, w_out, hp)
            ret_p.append(s_new)
            q, k, v, gate = _ret_proj(hs, g_mix, w_in, cos_s, sin_s)
            y, s_new = _ret_scan(q, k, v, gate, state_ret[j], ls, ret_chunk_s)
            hs = _outproj(y, w_out, hs)
            ret_s.append(s_new)
        else:
            w_qkv, w_out = sb_w_qkv[j].astype(BF16), sb_w_out[j].astype(BF16)
            q, kf, vf, kb, vb = _sb_proj(hp, g_mix, w_qkv)
            hp = _outproj(_sb_attn(q, kb, vb, sp), w_out, hp)
            kp_l.append(kf.reshape(bp, sp, H_SB, hd))
            vp_l.append(vf.reshape(bp, sp, H_SB, hd))
            q, kf, vf, kb, vb = _sb_proj(hs, g_mix, w_qkv)
            o = _sb_attn_sample(q, cache_sb_k[j].reshape(bs * past, d), cache_sb_v[j].reshape(bs * past, d),
                                kb, vb, ls)
            hs = _outproj(o, w_out, hs)
            ks_l.append(kf.reshape(bs, ls, H_SB, hd))
            vs_l.append(vf.reshape(bs, ls, H_SB, hd))
        wb_gu, wb_dn = w_ffn_b_gu[i].astype(BF16), w_ffn_b_down[i].astype(BF16)
        hp = _ffn(hp, row(norm_ffn_b[i]), wb_gu, wb_dn)
        hs = _ffn(hs, row(norm_ffn_b[i]), wb_gu, wb_dn)
        w_gate, w_proj = ple_w_gate[i].astype(BF16), ple_w_proj[i].astype(BF16)
        final = i == depth - 1
        hp = _ple(hp, pp[i], row(norm_ple[i]), w_gate, w_proj, g_final, final)
        hs = _ple(hs, ps[i], row(norm_ple[i]), w_gate, w_proj, g_final, final)

    return (hp.reshape(bp, sp, d), hs.reshape(bs, ls, d),
            jnp.stack(ret_p), jnp.stack(ret_s),
            jnp.stack(kp_l), jnp.stack(vp_l), jnp.stack(ks_l), jnp.stack(vs_l))
```

```python
import functools

import jax
import jax.numpy as jnp
from jax import lax
from jax.experimental import pallas as pl
from jax.experimental.pallas import tpu as pltpu

F32 = jnp.float32
BF16 = jnp.bfloat16

H_RET = 8
H_SB = 16
RET_CHUNK = 64
ROPE_BASE = 10000.0
EPS = 1e-6
GN_EPS = 1e-5

LANES = 128
VMEM_LIMIT_BYTES = 56 * 1024 * 1024
TOKEN_TILE = 512
FFN_COLS = 1024
RET_BLOCK = 256
SB_BLOCK = 256
SB_SAMPLE_LANES = 512
SB_SKIP_MASS = 104.0


def _params(*semantics):
    return pltpu.CompilerParams(dimension_semantics=semantics, vmem_limit_bytes=VMEM_LIMIT_BYTES)


def _rmsnorm(x, g):
    return x * lax.rsqrt(jnp.mean(x * x, axis=-1, keepdims=True) + EPS) * g


def _dot(a, b):
    return jnp.dot(a, b, preferred_element_type=F32)


def _dot_nt(a, b):
    return lax.dot_general(a, b, (((1,), (1,)), ((), ())), preferred_element_type=F32)


def _dot_tn(a, b):
    return lax.dot_general(a, b, (((0,), (0,)), ((), ())), preferred_element_type=F32)


def _ffn_kernel(x_ref, g_ref, wgu_ref, wd_ref, o_ref, acc_ref):
    d_ff = wd_ref.shape[0]
    x = x_ref[...]
    xn = _rmsnorm(x, g_ref[...]).astype(BF16)
    for c0 in range(0, d_ff, FFN_COLS):
        cw = min(FFN_COLS, d_ff - c0)
        a = _dot(xn, wgu_ref[:, c0:c0 + cw])
        b = _dot(xn, wgu_ref[:, d_ff + c0:d_ff + c0 + cw])
        h = (a * jax.nn.sigmoid(a) * b).astype(BF16)
        part = _dot(h, wd_ref[c0:c0 + cw, :])
        if c0 == 0:
            acc_ref[...] = part
        else:
            acc_ref[...] += part
    o_ref[...] = x + 0.5 * acc_ref[...]


def _resident(shape):
    return pl.BlockSpec(shape, lambda *_: (0,) * len(shape), pipeline_mode=pl.Buffered(1))


def _ffn(x, g, w_gu, w_down):
    t, d = x.shape
    d_ff = w_down.shape[0]
    tm = min(TOKEN_TILE, t)
    assert t % tm == 0 and w_gu.shape == (d, 2 * d_ff) and d_ff % LANES == 0
    return pl.pallas_call(
        _ffn_kernel,
        grid=(t // tm,),
        in_specs=[
            pl.BlockSpec((tm, d), lambda i: (i, 0)),
            _resident((1, d)),
            _resident((d, 2 * d_ff)),
            _resident((d_ff, d)),
        ],
        out_specs=pl.BlockSpec((tm, d), lambda i: (i, 0)),
        out_shape=jax.ShapeDtypeStruct((t, d), F32),
        scratch_shapes=[pltpu.VMEM((tm, d), F32)],
        compiler_params=_params("parallel"),
        name="ffn",
    )(x, g, w_gu, w_down)


def _ple_kernel(x_ref, p_ref, g_ref, wg_ref, wp_ref, gf_ref, o_ref, *, final):
    x = x_ref[...]
    xn = _rmsnorm(x, g_ref[...]).astype(BF16)
    gate = jax.nn.sigmoid(_dot(xn, wg_ref[...]))
    y = x + gate * _dot(p_ref[...].astype(BF16), wp_ref[...])
    if final:
        y = _rmsnorm(y, gf_ref[...])
    o_ref[...] = y


def _ple(x, p, g, w_gate, w_proj, g_final, final):
    t, d = x.shape
    pd = p.shape[1]
    tm = min(TOKEN_TILE, t)
    assert t % tm == 0
    return pl.pallas_call(
        functools.partial(_ple_kernel, final=final),
        grid=(t // tm,),
        in_specs=[
            pl.BlockSpec((tm, d), lambda i: (i, 0)),
            pl.BlockSpec((tm, pd), lambda i: (i, 0)),
            _resident((1, d)),
            _resident((d, d)),
            _resident((pd, d)),
            _resident((1, d)),
        ],
        out_specs=pl.BlockSpec((tm, d), lambda i: (i, 0)),
        out_shape=jax.ShapeDtypeStruct((t, d), F32),
        compiler_params=_params("parallel"),
        name="ple",
    )(x, p, g, w_gate, w_proj, g_final)


def _outproj_kernel(a_ref, w_ref, x_ref, o_ref):
    o_ref[...] = x_ref[...] + _dot(a_ref[...], w_ref[...])


def _outproj(a, w, x):
    t, d = x.shape
    kdim = a.shape[1]
    tm = min(TOKEN_TILE, t)
    assert t % tm == 0
    return pl.pallas_call(
        _outproj_kernel,
        grid=(t // tm,),
        in_specs=[
            pl.BlockSpec((tm, kdim), lambda i: (i, 0)),
            _resident((kdim, d)),
            pl.BlockSpec((tm, d), lambda i: (i, 0)),
        ],
        out_specs=pl.BlockSpec((tm, d), lambda i: (i, 0)),
        out_shape=jax.ShapeDtypeStruct((t, d), F32),
        compiler_params=_params("parallel"),
        name="outproj",
    )(a, w, x)


def _ret_proj_kernel(x_ref, g_ref, w_ref, cos_ref, sin_ref, q_ref, k_ref, v_ref, gate_ref):
    d = x_ref.shape[1]
    xn = _rmsnorm(x_ref[...], g_ref[...]).astype(BF16)
    cos = cos_ref[...]
    sin = sin_ref[...]
    for part, out in ((0, q_ref), (1, k_ref)):
        r = _dot(xn, w_ref[:, part * d:(part + 1) * d])
        for h in range(d // LANES):
            rh = r[:, h * LANES:(h + 1) * LANES]
            out[:, h * LANES:(h + 1) * LANES] = (rh * cos + pltpu.roll(rh, LANES // 2, 1) * sin).astype(BF16)
    for c in range(2):
        v_ref[:, c * d:(c + 1) * d] = _dot(xn, w_ref[:, (2 + c) * d:(3 + c) * d]).astype(BF16)
        gate_ref[:, c * d:(c + 1) * d] = _dot(xn, w_ref[:, (4 + c) * d:(5 + c) * d])


def _ret_proj(x, g, w_in, cos, sin):
    t, d = x.shape
    tm = min(TOKEN_TILE, t)
    nrot = cos.shape[0] // tm
    assert t % tm == 0 and cos.shape[0] % tm == 0 and w_in.shape[1] == 6 * d
    return pl.pallas_call(
        _ret_proj_kernel,
        grid=(t // tm,),
        in_specs=[
            pl.BlockSpec((tm, d), lambda i: (i, 0)),
            _resident((1, d)),
            _resident((d, 6 * d)),
            pl.BlockSpec((tm, LANES), lambda i: (i % nrot, 0)),
            pl.BlockSpec((tm, LANES), lambda i: (i % nrot, 0)),
        ],
        out_specs=[
            pl.BlockSpec((tm, d), lambda i: (i, 0)),
            pl.BlockSpec((tm, d), lambda i: (i, 0)),
            pl.BlockSpec((tm, 2 * d), lambda i: (i, 0)),
            pl.BlockSpec((tm, 2 * d), lambda i: (i, 0)),
        ],
        out_shape=[
            jax.ShapeDtypeStruct((t, d), BF16),
            jax.ShapeDtypeStruct((t, d), BF16),
            jax.ShapeDtypeStruct((t, 2 * d), BF16),
            jax.ShapeDtypeStruct((t, 2 * d), F32),
        ],
        compiler_params=_params("parallel"),
        name="ret_proj",
    )(x, g, w_in, cos, sin)


def _ret_scan_kernel(q_ref, k_ref, v_ref, g_ref, s0_ref, din_ref, dq_ref, dk_ref, dc_ref,
                     y_ref, so_ref, s_ref):
    c = pl.program_id(1)
    dk_w = q_ref.shape[1] // H_RET
    dv_w = v_ref.shape[1] // H_RET

    @pl.when(c == 0)
    def _():
        s_ref[...] = s0_ref[0]

    for h in range(H_RET):
        q = q_ref[:, h * dk_w:(h + 1) * dk_w]
        k = k_ref[:, h * dk_w:(h + 1) * dk_w]
        v = v_ref[:, h * dv_w:(h + 1) * dv_w]
        s = s_ref[h]
        scores = _dot_nt(q, k) * din_ref[h]
        o = _dot(scores.astype(BF16), v) + _dot(q, s.astype(BF16)) * dq_ref[h]
        k_dec = (k.astype(F32) * dk_ref[h]).astype(BF16)
        s_ref[h] = dc_ref[h] * s + _dot_tn(k_dec, v)
        mu = jnp.mean(o, axis=-1, keepdims=True)
        oc = o - mu
        var = jnp.mean(oc * oc, axis=-1, keepdims=True)
        on = oc * lax.rsqrt(var + GN_EPS)
        gt = g_ref[:, h * dv_w:(h + 1) * dv_w]
        y_ref[:, h * dv_w:(h + 1) * dv_w] = (gt * jax.nn.sigmoid(gt) * on).astype(BF16)

    @pl.when(c == pl.num_programs(1) - 1)
    def _():
        so_ref[0] = s_ref[...]


def _ret_tables(c, dk_w, dv_w):
    scale = dk_w ** -0.5
    log_g = jnp.log1p(-(2.0 ** (-5.0 - jnp.arange(H_RET, dtype=F32))))
    idx = jnp.arange(c, dtype=F32)
    rel = idx[:, None] - idx[None, :]
    din = jnp.where(rel[None] >= 0, jnp.exp(jnp.maximum(rel, 0.0)[None] * log_g[:, None, None]), 0.0) * scale
    dq = jnp.exp((idx + 1.0)[None, :] * log_g[:, None])
    dk = jnp.exp((c - 1.0 - idx)[None, :] * log_g[:, None]) * scale
    dc = jnp.exp(c * log_g)
    dq = jnp.broadcast_to(dq[:, :, None], (H_RET, c, dv_w))
    dk = jnp.broadcast_to(dk[:, :, None], (H_RET, c, dk_w))
    dc = jnp.broadcast_to(dc[:, None, None], (H_RET, 1, dv_w))
    return din, dq, dk, dc


def _ret_scan(q, k, v, gate, s0, seq_len, chunk):
    t, dqk = q.shape
    dvv = v.shape[1]
    b = t // seq_len
    nc = seq_len // chunk
    dk_w, dv_w = dqk // H_RET, dvv // H_RET
    assert nc * chunk == seq_len and s0.shape == (b, H_RET, dk_w, dv_w)
    din, dq, dk, dc = _ret_tables(chunk, dk_w, dv_w)
    tok = lambda bi, ci: (bi * nc + ci, 0)
    state = lambda bi, ci: (bi, 0, 0, 0)
    return pl.pallas_call(
        _ret_scan_kernel,
        grid=(b, nc),
        in_specs=[
            pl.BlockSpec((chunk, dqk), tok),
            pl.BlockSpec((chunk, dqk), tok),
            pl.BlockSpec((chunk, dvv), tok),
            pl.BlockSpec((chunk, dvv), tok),
            pl.BlockSpec((1, H_RET, dk_w, dv_w), state),
            _resident((H_RET, chunk, chunk)),
            _resident((H_RET, chunk, dv_w)),
            _resident((H_RET, chunk, dk_w)),
            _resident((H_RET, 1, dv_w)),
        ],
        out_specs=[
            pl.BlockSpec((chunk, dvv), tok),
            pl.BlockSpec((1, H_RET, dk_w, dv_w), state),
        ],
        out_shape=[
            jax.ShapeDtypeStruct((t, dvv), BF16),
            jax.ShapeDtypeStruct((b, H_RET, dk_w, dv_w), F32),
        ],
        scratch_shapes=[pltpu.VMEM((H_RET, dk_w, dv_w), F32)],
        compiler_params=_params("parallel", "arbitrary"),
        name="ret_scan",
    )(q, k, v, gate, s0, din, dq, dk, dc)


def _sb_proj_kernel(x_ref, g_ref, w_ref, q_ref, kf_ref, vf_ref, kb_ref, vb_ref, *, scale):
    d = x_ref.shape[1]
    xn = _rmsnorm(x_ref[...], g_ref[...]).astype(BF16)
    q_ref[...] = (_dot(xn, w_ref[:, 0:d]) * scale).astype(BF16)
    k = _dot(xn, w_ref[:, d:2 * d])
    kf_ref[...] = k
    kb_ref[...] = k.astype(BF16)
    v = _dot(xn, w_ref[:, 2 * d:3 * d])
    vf_ref[...] = v
    vb_ref[...] = v.astype(BF16)


def _sb_proj(x, g, w_qkv):
    t, d = x.shape
    tm = min(TOKEN_TILE, t)
    assert t % tm == 0 and w_qkv.shape[1] == 3 * d
    scale = (d // H_SB) ** -0.5
    tok = pl.BlockSpec((tm, d), lambda i: (i, 0))
    return pl.pallas_call(
        functools.partial(_sb_proj_kernel, scale=scale),
        grid=(t // tm,),
        in_specs=[
            tok,
            _resident((1, d)),
            _resident((d, 3 * d)),
        ],
        out_specs=[tok, tok, tok, tok, tok],
        out_shape=[
            jax.ShapeDtypeStruct((t, d), BF16),
            jax.ShapeDtypeStruct((t, d), F32),
            jax.ShapeDtypeStruct((t, d), F32),
            jax.ShapeDtypeStruct((t, d), BF16),
            jax.ShapeDtypeStruct((t, d), BF16),
        ],
        compiler_params=_params("parallel"),
        name="sb_proj",
    )(x, g, w_qkv)


def _softplus(z):
    return jnp.maximum(z, 0.0) + jnp.log(1.0 + jnp.exp(-jnp.abs(z)))


def _sb_chains(chains, carry, acc):
    carry, acc = list(carry), list(acc)
    zs = [_dot_nt(q, k) for _, q, k, _, _, _ in chains]
    sps = []
    for z, (_, _, _, _, _, mask) in zip(zs, chains):
        sp = _softplus(z)
        sps.append(sp if mask is None else jnp.where(mask, sp, 0.0))
    newers = []
    for sp, (_, _, _, _, tri, _) in zip(sps, chains):
        hi = sp.astype(BF16)
        lo = (sp - hi.astype(F32)).astype(BF16)
        newers.append(_dot(hi, tri) + _dot(lo, tri))
    weights = []
    for z, sp, newer, (h, _, _, _, _, mask) in zip(zs, sps, newers, chains):
        a = jnp.exp(z - sp - newer - carry[h])
        weights.append((a if mask is None else jnp.where(mask, a, 0.0)).astype(BF16))
        carry[h] = carry[h] + jnp.sum(sp, axis=-1, keepdims=True)
    for a, (h, _, _, v, _, _) in zip(weights, chains):
        acc[h] = acc[h] + _dot(a, v)
    return carry, acc


def _min_mass(carry):
    m = jnp.min(carry[0])
    for c in carry[1:]:
        m = jnp.minimum(m, jnp.min(c))
    return m


def _sb_older_blocks(chains_of, first_kb, carry, acc):
    n = len(carry)

    def cond(state):
        return jnp.logical_and(state[0] >= 0, state[1] < SB_SKIP_MASS)

    def body(state):
        kb = state[0]
        c, a = _sb_chains(chains_of(kb), state[2:2 + n], state[2 + n:])
        return (kb - 1, _min_mass(c), *c, *a)

    state = lax.while_loop(cond, body, (first_kb, _min_mass(carry), *carry, *acc))
    return state[2 + n:]


def _strict_tri(n):
    r = lax.broadcasted_iota(jnp.int32, (n, n), 0)
    c = lax.broadcasted_iota(jnp.int32, (n, n), 1)
    return r, c


def _sb_attn_kernel(q_ref, k_ref, v_ref, tri_ref, o_ref):
    qi = pl.program_id(2)
    tq = q_ref.shape[0]
    hd = q_ref.shape[1] // 2
    heads = (0, 1)
    lanes = [slice(h * hd, (h + 1) * hd) for h in heads]
    tri = tri_ref[...]
    row, col = _strict_tri(tq)
    diag_mask = col < row
    qs = [q_ref[:, lanes[h]] for h in heads]

    def chains_of(kb, mask=None):
        ks = pl.multiple_of(kb * tq, tq)
        return [(h, qs[h], k_ref[pl.ds(ks, tq), lanes[h]], v_ref[pl.ds(ks, tq), lanes[h]], tri, mask)
                for h in heads]

    carry0 = [jnp.zeros((tq, 1), F32) for _ in heads]
    acc0 = [jnp.zeros((tq, hd), F32) for _ in heads]

    def store(acc):
        for h in heads:
            o_ref[:, lanes[h]] = acc[h].astype(BF16)

    @pl.when(qi == 0)
    def _():
        store(_sb_chains(chains_of(0, diag_mask), carry0, acc0)[1])

    @pl.when(qi > 0)
    def _():
        carry, acc = _sb_chains(chains_of(qi, diag_mask) + chains_of(qi - 1), carry0, acc0)
        store(_sb_older_blocks(chains_of, qi - 2, carry, acc))


def _tri_matrix(n):
    r, c = _strict_tri(n)
    return (r > c).astype(BF16)


def _sb_attn(q, k, v, seq_len):
    t, d = q.shape
    b = t // seq_len
    tq = min(SB_BLOCK, seq_len)
    nq = seq_len // tq
    hp = 2 * (d // H_SB)
    assert nq * tq == seq_len and hp == LANES
    qmap = lambda bi, hi, qi: (bi * nq + qi, hi)
    kvmap = lambda bi, hi, qi: (bi, hi)
    return pl.pallas_call(
        _sb_attn_kernel,
        grid=(b, d // hp, nq),
        in_specs=[
            pl.BlockSpec((tq, hp), qmap),
            pl.BlockSpec((seq_len, hp), kvmap),
            pl.BlockSpec((seq_len, hp), kvmap),
            _resident((tq, tq)),
        ],
        out_specs=pl.BlockSpec((tq, hp), qmap),
        out_shape=jax.ShapeDtypeStruct((t, d), BF16),
        compiler_params=_params("parallel", "parallel", "arbitrary"),
        name="sb_attn",
    )(q, k, v, _tri_matrix(tq))


def _sb_attn_sample_kernel(q_ref, kc_ref, vc_ref, kn_ref, vn_ref, tri_ref, trin_ref, o_ref, *, tk, hd):
    ls = q_ref.shape[0]
    heads = tuple(range(q_ref.shape[1] // hd))
    lanes = [slice(h * hd, (h + 1) * hd) for h in heads]
    last_kb = kc_ref.shape[0] // tk - 1
    tri = tri_ref[...]
    trin = trin_ref[...]
    row, col = _strict_tri(ls)
    new_mask = col < row
    qs = [q_ref[:, lanes[h]] for h in heads]

    def cache_chains(kb):
        rows = pl.ds(pl.multiple_of(kb * tk, tk), tk)
        return [(h, qs[h], kc_ref[rows, lanes[h]].astype(BF16), vc_ref[rows, lanes[h]].astype(BF16), tri, None)
                for h in heads]

    new_chains = [(h, qs[h], kn_ref[:, lanes[h]], vn_ref[:, lanes[h]], trin, new_mask) for h in heads]
    carry = [jnp.zeros((ls, 1), F32) for _ in heads]
    acc = [jnp.zeros((ls, hd), F32) for _ in heads]
    carry, acc = _sb_chains(new_chains + cache_chains(last_kb), carry, acc)
    acc = _sb_older_blocks(cache_chains, last_kb - 1, carry, acc)
    for h in heads:
        o_ref[:, lanes[h]] = acc[h].astype(BF16)


def _sb_attn_sample(q, k_cache, v_cache, k_new, v_new, ls):
    t, d = q.shape
    b = t // ls
    past = k_cache.shape[0] // b
    tk = min(SB_BLOCK, past)
    hp = min(SB_SAMPLE_LANES, d)
    assert past % tk == 0 and d % hp == 0 and hp % LANES == 0
    new = pl.BlockSpec((ls, hp), lambda bi, hi: (bi, hi))
    old = pl.BlockSpec((past, hp), lambda bi, hi: (bi, hi))
    return pl.pallas_call(
        functools.partial(_sb_attn_sample_kernel, tk=tk, hd=d // H_SB),
        grid=(b, d // hp),
        in_specs=[
            new, old, old, new, new,
            _resident((tk, tk)),
            _resident((ls, ls)),
        ],
        out_specs=new,
        out_shape=jax.ShapeDtypeStruct((t, d), BF16),
        compiler_params=_params("parallel", "parallel"),
        name="sb_attn_sample",
    )(q, k_cache, v_cache, k_new, v_new, _tri_matrix(tk), _tri_matrix(ls))


def _rotary_tables(pos, half, reps):
    freqs = ROPE_BASE ** (-jnp.arange(half, dtype=F32) / half)
    ang = pos.astype(F32)[:, None] * freqs[None, :]
    cos, sin = jnp.cos(ang), jnp.sin(ang)
    cos = jnp.concatenate([cos, cos], axis=-1)
    sin = jnp.concatenate([-sin, sin], axis=-1)
    return jnp.tile(cos, (reps, 1)), jnp.tile(sin, (reps, 1))


def kernel(x_prompt, x_sample, p_prompt, p_sample, state_ret, cache_sb_k, cache_sb_v, norm_ffn_a, w_ffn_a_gu, w_ffn_a_down, norm_mix, ret_w_in, ret_w_out, sb_w_qkv, sb_w_out, norm_ffn_b, w_ffn_b_gu, w_ffn_b_down, norm_ple, ple_w_gate, ple_w_proj, norm_final):
    bp, sp, d = x_prompt.shape
    bs, ls, _ = x_sample.shape
    depth = norm_ffn_a.shape[0]
    past = cache_sb_k.shape[2]
    dk_w = d // H_RET
    dv_w = ret_w_out.shape[1] // H_RET
    hd = d // H_SB

    hp = x_prompt.reshape(bp * sp, d)
    hs = x_sample.reshape(bs * ls, d)
    pp = p_prompt.reshape(depth, bp * sp, -1)
    ps = p_sample.reshape(depth, bs * ls, -1)
    row = lambda g: g.reshape(1, d)

    cos_p, sin_p = _rotary_tables(jnp.arange(sp), dk_w // 2, 1)
    cos_s, sin_s = _rotary_tables(past + jnp.arange(ls), dk_w // 2, bs)
    ret_chunk_p = min(RET_BLOCK, sp)
    ret_chunk_s = min(RET_CHUNK, ls)
    zero_state = jnp.zeros((bp, H_RET, dk_w, dv_w), F32)
    g_final = row(norm_final)

    ret_p, ret_s, kp_l, vp_l, ks_l, vs_l = [], [], [], [], [], []
    for i in range(depth):
        j = i // 2
        wa_gu, wa_dn = w_ffn_a_gu[i].astype(BF16), w_ffn_a_down[i].astype(BF16)
        hp = _ffn(hp, row(norm_ffn_a[i]), wa_gu, wa_dn)
        hs = _ffn(hs, row(norm_ffn_a[i]), wa_gu, wa_dn)
        g_mix = row(norm_mix[i])
        if i % 2 == 0:
            w_in, w_out = ret_w_in[j].astype(BF16), ret_w_out[j].astype(BF16)
            q, k, v, gate = _ret_proj(hp, g_mix, w_in, cos_p, sin_p)
            y, s_new = _ret_scan(q, k, v, gate, zero_state, sp, ret_chunk_p)
            hp = _outproj(y, w_out, hp)
            ret_p.append(s_new)
            q, k, v, gate = _ret_proj(hs, g_mix, w_in, cos_s, sin_s)
            y, s_new = _ret_scan(q, k, v, gate, state_ret[j], ls, ret_chunk_s)
            hs = _outproj(y, w_out, hs)
            ret_s.append(s_new)
        else:
            w_qkv, w_out = sb_w_qkv[j].astype(BF16), sb_w_out[j].astype(BF16)
            q, kf, vf, kb, vb = _sb_proj(hp, g_mix, w_qkv)
            hp = _outproj(_sb_attn(q, kb, vb, sp), w_out, hp)
            kp_l.append(kf.reshape(bp, sp, H_SB, hd))
            vp_l.append(vf.reshape(bp, sp, H_SB, hd))
            q, kf, vf, kb, vb = _sb_proj(hs, g_mix, w_qkv)
            o = _sb_attn_sample(q, cache_sb_k[j].reshape(bs * past, d), cache_sb_v[j].reshape(bs * past, d),
                                kb, vb, ls)
            hs = _outproj(o, w_out, hs)
            ks_l.append(kf.reshape(bs, ls, H_SB, hd))
            vs_l.append(vf.reshape(bs, ls, H_SB, hd))
        wb_gu, wb_dn = w_ffn_b_gu[i].astype(BF16), w_ffn_b_down[i].astype(BF16)
        hp = _ffn(hp, row(norm_ffn_b[i]), wb_gu, wb_dn)
        hs = _ffn(hs, row(norm_ffn_b[i]), wb_gu, wb_dn)
        w_gate, w_proj = ple_w_gate[i].astype(BF16), ple_w_proj[i].astype(BF16)
        final = i == depth - 1
        hp = _ple(hp, pp[i], row(norm_ple[i]), w_gate, w_proj, g_final, final)
        hs = _ple(hs, ps[i], row(norm_ple[i]), w_gate, w_proj, g_final, final)

    return (hp.reshape(bp, sp, d), hs.reshape(bs, ls, d),
            jnp.stack(ret_p), jnp.stack(ret_s),
            jnp.stack(kp_l), jnp.stack(vp_l), jnp.stack(ks_l), jnp.stack(vs_l))
```

```python
import functools

import jax
import jax.numpy as jnp
from jax import lax
from jax.experimental import pallas as pl
from jax.experimental.pallas import tpu as pltpu

F32 = jnp.float32
BF16 = jnp.bfloat16

H_RET = 8
H_SB = 16
RET_CHUNK = 64
ROPE_BASE = 10000.0
EPS = 1e-6
GN_EPS = 1e-5

LANES = 128
SUBLANES = 8
VMEM_LIMIT_BYTES = 56 * 1024 * 1024
TOKEN_TILE = 512
FFN_COLS = 1024
RET_BLOCK = 256
SB_BLOCK = 256
SB_PROMPT_LANES = 256
SB_SAMPLE_LANES = 512
SB_SKIP_MASS = 104.0


def _params(*semantics):
    return pltpu.CompilerParams(dimension_semantics=semantics, vmem_limit_bytes=VMEM_LIMIT_BYTES)


def _rmsnorm(x, g):
    return x * lax.rsqrt(jnp.mean(x * x, axis=-1, keepdims=True) + EPS) * g


def _dot(a, b):
    return jnp.dot(a, b, preferred_element_type=F32)


def _dot_nt(a, b):
    return lax.dot_general(a, b, (((1,), (1,)), ((), ())), preferred_element_type=F32)


def _dot_tn(a, b):
    return lax.dot_general(a, b, (((0,), (0,)), ((), ())), preferred_element_type=F32)


def _ffn_body(x, g_ref, wgu_ref, wd_ref, acc_ref):
    d_ff = wd_ref.shape[0]
    xn = _rmsnorm(x, g_ref[...]).astype(BF16)
    for c0 in range(0, d_ff, FFN_COLS):
        cw = min(FFN_COLS, d_ff - c0)
        a = _dot(xn, wgu_ref[:, c0:c0 + cw])
        b = _dot(xn, wgu_ref[:, d_ff + c0:d_ff + c0 + cw])
        h = (a * jax.nn.sigmoid(a) * b).astype(BF16)
        part = _dot(h, wd_ref[c0:c0 + cw, :])
        if c0 == 0:
            acc_ref[...] = part
        else:
            acc_ref[...] += part
    return x + 0.5 * acc_ref[...]


def _ffn_kernel(x_ref, g_ref, wgu_ref, wd_ref, o_ref, acc_ref):
    o_ref[...] = _ffn_body(x_ref[...], g_ref, wgu_ref, wd_ref, acc_ref)


def _resident(shape):
    return pl.BlockSpec(shape, lambda *_: (0,) * len(shape), pipeline_mode=pl.Buffered(1))


def _ffn(x, g, w_gu, w_down):
    t, d = x.shape
    d_ff = w_down.shape[0]
    tm = min(TOKEN_TILE, t)
    assert t % tm == 0 and w_gu.shape == (d, 2 * d_ff) and d_ff % LANES == 0
    return pl.pallas_call(
        _ffn_kernel,
        grid=(t // tm,),
        in_specs=[
            pl.BlockSpec((tm, d), lambda i: (i, 0)),
            _resident((1, d)),
            _resident((d, 2 * d_ff)),
            _resident((d_ff, d)),
        ],
        out_specs=pl.BlockSpec((tm, d), lambda i: (i, 0)),
        out_shape=jax.ShapeDtypeStruct((t, d), F32),
        scratch_shapes=[pltpu.VMEM((tm, d), F32)],
        compiler_params=_params("parallel"),
        name="ffn",
    )(x, g, w_gu, w_down)


def _post_mixer_kernel(a_ref, x_ref, p_ref, wo_ref, gb_ref, wgu_ref, wd_ref, gp_ref, wg_ref, wp_ref, gf_ref,
                       o_ref, acc_ref, *, final):
    x = x_ref[...] + _dot(a_ref[...], wo_ref[...])
    x = _ffn_body(x, gb_ref, wgu_ref, wd_ref, acc_ref)
    xn = _rmsnorm(x, gp_ref[...]).astype(BF16)
    gate = jax.nn.sigmoid(_dot(xn, wg_ref[...]))
    y = x + gate * _dot(p_ref[0].astype(BF16), wp_ref[...])
    if final:
        y = _rmsnorm(y, gf_ref[...])
    o_ref[...] = y


def _post_mixer(a, x, p_all, layer, w_out, g_ffn, w_gu, w_down, g_ple, w_gate, w_proj, g_final, final):
    t, d = x.shape
    kdim = a.shape[1]
    d_ff = w_down.shape[0]
    pd = p_all.shape[2]
    tm = min(TOKEN_TILE, t)
    assert t % tm == 0 and w_gu.shape == (d, 2 * d_ff) and d_ff % LANES == 0
    return pl.pallas_call(
        functools.partial(_post_mixer_kernel, final=final),
        grid=(t // tm,),
        in_specs=[
            pl.BlockSpec((tm, kdim), lambda i: (i, 0)),
            pl.BlockSpec((tm, d), lambda i: (i, 0)),
            pl.BlockSpec((1, tm, pd), lambda i: (layer, i, 0)),
            _resident((kdim, d)),
            _resident((1, d)),
            _resident((d, 2 * d_ff)),
            _resident((d_ff, d)),
            _resident((1, d)),
            _resident((d, d)),
            _resident((pd, d)),
            _resident((1, d)),
        ],
        out_specs=pl.BlockSpec((tm, d), lambda i: (i, 0)),
        out_shape=jax.ShapeDtypeStruct((t, d), F32),
        scratch_shapes=[pltpu.VMEM((tm, d), F32)],
        compiler_params=_params("parallel"),
        name="post_mixer",
    )(a, x, p_all, w_out, g_ffn, w_gu, w_down, g_ple, w_gate, w_proj, g_final)


def _ret_proj_kernel(x_ref, g_ref, w_ref, cos_ref, sin_ref, q_ref, k_ref, v_ref, gate_ref):
    d = x_ref.shape[1]
    xn = _rmsnorm(x_ref[...], g_ref[...]).astype(BF16)
    cos = cos_ref[...]
    sin = sin_ref[...]
    for part, out in ((0, q_ref), (1, k_ref)):
        r = _dot(xn, w_ref[:, part * d:(part + 1) * d])
        for h in range(d // LANES):
            rh = r[:, h * LANES:(h + 1) * LANES]
            out[:, h * LANES:(h + 1) * LANES] = (rh * cos + pltpu.roll(rh, LANES // 2, 1) * sin).astype(BF16)
    for c in range(2):
        v_ref[:, c * d:(c + 1) * d] = _dot(xn, w_ref[:, (2 + c) * d:(3 + c) * d]).astype(BF16)
        gate_ref[:, c * d:(c + 1) * d] = _dot(xn, w_ref[:, (4 + c) * d:(5 + c) * d])


def _ret_proj(x, g, w_in, cos, sin):
    t, d = x.shape
    tm = min(TOKEN_TILE, t)
    nrot = cos.shape[0] // tm
    assert t % tm == 0 and cos.shape[0] % tm == 0 and w_in.shape[1] == 6 * d
    return pl.pallas_call(
        _ret_proj_kernel,
        grid=(t // tm,),
        in_specs=[
            pl.BlockSpec((tm, d), lambda i: (i, 0)),
            _resident((1, d)),
            _resident((d, 6 * d)),
            pl.BlockSpec((tm, LANES), lambda i: (i % nrot, 0)),
            pl.BlockSpec((tm, LANES), lambda i: (i % nrot, 0)),
        ],
        out_specs=[
            pl.BlockSpec((tm, d), lambda i: (i, 0)),
            pl.BlockSpec((tm, d), lambda i: (i, 0)),
            pl.BlockSpec((tm, 2 * d), lambda i: (i, 0)),
            pl.BlockSpec((tm, 2 * d), lambda i: (i, 0)),
        ],
        out_shape=[
            jax.ShapeDtypeStruct((t, d), BF16),
            jax.ShapeDtypeStruct((t, d), BF16),
            jax.ShapeDtypeStruct((t, 2 * d), BF16),
            jax.ShapeDtypeStruct((t, 2 * d), F32),
        ],
        compiler_params=_params("parallel"),
        name="ret_proj",
    )(x, g, w_in, cos, sin)


def _ret_scan_kernel(q_ref, k_ref, v_ref, g_ref, s0_ref, din_ref, dq_ref, dk_ref, dc_ref,
                     y_ref, so_ref, s_ref):
    c = pl.program_id(1)
    dk_w = q_ref.shape[1] // H_RET
    dv_w = v_ref.shape[1] // H_RET

    @pl.when(c == 0)
    def _():
        s_ref[...] = s0_ref[0]

    heads = range(H_RET)
    qs = [q_ref[:, h * dk_w:(h + 1) * dk_w] for h in heads]
    ks = [k_ref[:, h * dk_w:(h + 1) * dk_w] for h in heads]
    vs = [v_ref[:, h * dv_w:(h + 1) * dv_w] for h in heads]
    ss = [s_ref[h] for h in heads]
    scores = [_dot_nt(qs[h], ks[h]) for h in heads]
    inter = [_dot(qs[h], ss[h].astype(BF16)) for h in heads]
    k_decs = [(ks[h].astype(F32) * dk_ref[h]).astype(BF16) for h in heads]
    updates = [_dot_tn(k_decs[h], vs[h]) for h in heads]
    for h in heads:
        s_ref[h] = dc_ref[h] * ss[h] + updates[h]
    probs = [(scores[h] * din_ref[h]).astype(BF16) for h in heads]
    outs = [_dot(probs[h], vs[h]) + inter[h] * dq_ref[h] for h in heads]
    for h in heads:
        o = outs[h]
        mu = jnp.mean(o, axis=-1, keepdims=True)
        oc = o - mu
        var = jnp.mean(oc * oc, axis=-1, keepdims=True)
        on = oc * lax.rsqrt(var + GN_EPS)
        gt = g_ref[:, h * dv_w:(h + 1) * dv_w]
        y_ref[:, h * dv_w:(h + 1) * dv_w] = (gt * jax.nn.sigmoid(gt) * on).astype(BF16)

    @pl.when(c == pl.num_programs(1) - 1)
    def _():
        so_ref[0] = s_ref[...]


def _ret_tables(c, dk_w, dv_w):
    scale = dk_w ** -0.5
    log_g = jnp.log1p(-(2.0 ** (-5.0 - jnp.arange(H_RET, dtype=F32))))
    idx = jnp.arange(c, dtype=F32)
    rel = idx[:, None] - idx[None, :]
    din = jnp.where(rel[None] >= 0, jnp.exp(jnp.maximum(rel, 0.0)[None] * log_g[:, None, None]), 0.0) * scale
    dq = jnp.exp((idx + 1.0)[None, :] * log_g[:, None])
    dk = jnp.exp((c - 1.0 - idx)[None, :] * log_g[:, None]) * scale
    dc = jnp.exp(c * log_g)
    dq = jnp.broadcast_to(dq[:, :, None], (H_RET, c, dv_w))
    dk = jnp.broadcast_to(dk[:, :, None], (H_RET, c, dk_w))
    dc = jnp.broadcast_to(dc[:, None, None], (H_RET, 1, dv_w))
    return din, dq, dk, dc


def _ret_scan(q, k, v, gate, s0, seq_len, chunk):
    t, dqk = q.shape
    dvv = v.shape[1]
    b = t // seq_len
    nc = seq_len // chunk
    dk_w, dv_w = dqk // H_RET, dvv // H_RET
    assert nc * chunk == seq_len and s0.shape == (b, H_RET, dk_w, dv_w)
    din, dq, dk, dc = _ret_tables(chunk, dk_w, dv_w)
    tok = lambda bi, ci: (bi * nc + ci, 0)
    state = lambda bi, ci: (bi, 0, 0, 0)
    return pl.pallas_call(
        _ret_scan_kernel,
        grid=(b, nc),
        in_specs=[
            pl.BlockSpec((chunk, dqk), tok),
            pl.BlockSpec((chunk, dqk), tok),
            pl.BlockSpec((chunk, dvv), tok),
            pl.BlockSpec((chunk, dvv), tok),
            pl.BlockSpec((1, H_RET, dk_w, dv_w), state),
            _resident((H_RET, chunk, chunk)),
            _resident((H_RET, chunk, dv_w)),
            _resident((H_RET, chunk, dk_w)),
            _resident((H_RET, 1, dv_w)),
        ],
        out_specs=[
            pl.BlockSpec((chunk, dvv), tok),
            pl.BlockSpec((1, H_RET, dk_w, dv_w), state),
        ],
        out_shape=[
            jax.ShapeDtypeStruct((t, dvv), BF16),
            jax.ShapeDtypeStruct((b, H_RET, dk_w, dv_w), F32),
        ],
        scratch_shapes=[pltpu.VMEM((H_RET, dk_w, dv_w), F32)],
        compiler_params=_params("parallel", "arbitrary"),
        name="ret_scan",
    )(q, k, v, gate, s0, din, dq, dk, dc)


def _sb_proj_kernel(x_ref, g_ref, w_ref, k_all_ref, v_all_ref, q_ref, kf_ref, vf_ref, kb_ref, vb_ref, *, scale):
    del k_all_ref, v_all_ref
    d = x_ref.shape[1]
    xn = _rmsnorm(x_ref[...], g_ref[...]).astype(BF16)
    q_ref[...] = (_dot(xn, w_ref[:, 0:d]) * scale).astype(BF16)
    k = _dot(xn, w_ref[:, d:2 * d])
    kf_ref[0] = k
    kb_ref[...] = k.astype(BF16)
    v = _dot(xn, w_ref[:, 2 * d:3 * d])
    vf_ref[0] = v
    vb_ref[...] = v.astype(BF16)


def _sb_proj(x, g, w_qkv, k_all, v_all, layer):
    t, d = x.shape
    tm = min(TOKEN_TILE, t)
    assert t % tm == 0 and w_qkv.shape[1] == 3 * d and k_all.shape[1:] == (t, d)
    scale = (d // H_SB) ** -0.5
    tok = pl.BlockSpec((tm, d), lambda i: (i, 0))
    slab = pl.BlockSpec((1, tm, d), lambda i: (layer, i, 0))
    hbm = pl.BlockSpec(memory_space=pl.ANY)
    return pl.pallas_call(
        functools.partial(_sb_proj_kernel, scale=scale),
        grid=(t // tm,),
        in_specs=[
            tok,
            _resident((1, d)),
            _resident((d, 3 * d)),
            hbm,
            hbm,
        ],
        out_specs=[tok, slab, slab, tok, tok],
        out_shape=[
            jax.ShapeDtypeStruct((t, d), BF16),
            jax.ShapeDtypeStruct(k_all.shape, F32),
            jax.ShapeDtypeStruct(v_all.shape, F32),
            jax.ShapeDtypeStruct((t, d), BF16),
            jax.ShapeDtypeStruct((t, d), BF16),
        ],
        input_output_aliases={3: 1, 4: 2},
        compiler_params=_params("parallel"),
        name="sb_proj",
    )(x, g, w_qkv, k_all, v_all)


def _softplus(z):
    return jnp.maximum(z, 0.0) + jnp.log(1.0 + jnp.exp(-jnp.abs(z)))


def _sb_chains(chains, carry, acc):
    carry, acc = list(carry), list(acc)
    zs = [_dot_nt(q, k) for _, q, k, _, _, _ in chains]
    sps = []
    for z, (_, _, _, _, _, mask) in zip(zs, chains):
        sp = _softplus(z)
        sps.append(sp if mask is None else jnp.where(mask, sp, 0.0))
    newers = []
    for sp, (_, _, _, _, tri, _) in zip(sps, chains):
        hi = sp.astype(BF16)
        lo = (sp - hi.astype(F32)).astype(BF16)
        newers.append(_dot(hi, tri) + _dot(lo, tri))
    weights = []
    for z, sp, newer, (h, _, _, _, _, mask) in zip(zs, sps, newers, chains):
        a = jnp.exp(z - sp - newer - carry[h])
        weights.append((a if mask is None else jnp.where(mask, a, 0.0)).astype(BF16))
        carry[h] = carry[h] + jnp.sum(sp, axis=-1, keepdims=True)
    for a, (h, _, _, v, _, _) in zip(weights, chains):
        acc[h] = acc[h] + _dot(a, v)
    return carry, acc


def _min_mass(carry):
    m = jnp.min(carry[0])
    for c in carry[1:]:
        m = jnp.minimum(m, jnp.min(c))
    return m


def _sb_older_blocks(chains_of, first_kb, carry, acc):
    n = len(carry)

    def cond(state):
        return jnp.logical_and(state[0] >= 0, state[1] < SB_SKIP_MASS)

    def body(state):
        kb = state[0]
        c, a = _sb_chains(chains_of(kb), state[2:2 + n], state[2 + n:])
        return (kb - 1, _min_mass(c), *c, *a)

    state = lax.while_loop(cond, body, (first_kb, _min_mass(carry), *carry, *acc))
    return state[2 + n:]


def _strict_tri(n):
    r = lax.broadcasted_iota(jnp.int32, (n, n), 0)
    c = lax.broadcasted_iota(jnp.int32, (n, n), 1)
    return r, c


def _sb_attn_kernel(q_ref, k_ref, v_ref, tri_ref, o_ref, *, hd):
    qi = pl.program_id(2)
    tq = q_ref.shape[0]
    heads = tuple(range(q_ref.shape[1] // hd))
    lanes = [slice(h * hd, (h + 1) * hd) for h in heads]
    tri = tri_ref[...]
    row, col = _strict_tri(tq)
    diag_mask = col < row
    qs = [q_ref[:, lanes[h]] for h in heads]

    def chains_of(kb, mask=None):
        ks = pl.multiple_of(kb * tq, tq)
        return [(h, qs[h], k_ref[pl.ds(ks, tq), lanes[h]], v_ref[pl.ds(ks, tq), lanes[h]], tri, mask)
                for h in heads]

    carry0 = [jnp.zeros((tq, 1), F32) for _ in heads]
    acc0 = [jnp.zeros((tq, hd), F32) for _ in heads]

    def store(acc):
        for h in heads:
            o_ref[:, lanes[h]] = acc[h].astype(BF16)

    @pl.when(qi == 0)
    def _():
        store(_sb_chains(chains_of(0, diag_mask), carry0, acc0)[1])

    @pl.when(qi > 0)
    def _():
        carry, acc = _sb_chains(chains_of(qi, diag_mask) + chains_of(qi - 1), carry0, acc0)
        store(_sb_older_blocks(chains_of, qi - 2, carry, acc))


def _tri_matrix(n):
    r, c = _strict_tri(n)
    return (r > c).astype(BF16)


def _sb_attn(q, k, v, seq_len):
    t, d = q.shape
    b = t // seq_len
    tq = min(SB_BLOCK, seq_len)
    nq = seq_len // tq
    hp = min(SB_PROMPT_LANES, d)
    assert nq * tq == seq_len and d % hp == 0 and hp % LANES == 0
    qmap = lambda bi, hi, qi: (bi * nq + qi, hi)
    kvmap = lambda bi, hi, qi: (bi, hi)
    return pl.pallas_call(
        functools.partial(_sb_attn_kernel, hd=d // H_SB),
        grid=(b, d // hp, nq),
        in_specs=[
            pl.BlockSpec((tq, hp), qmap),
            pl.BlockSpec((seq_len, hp), kvmap),
            pl.BlockSpec((seq_len, hp), kvmap),
            _resident((tq, tq)),
        ],
        out_specs=pl.BlockSpec((tq, hp), qmap),
        out_shape=jax.ShapeDtypeStruct((t, d), BF16),
        compiler_params=_params("parallel", "parallel", "arbitrary"),
        name="sb_attn",
    )(q, k, v, _tri_matrix(tq))


def _sb_attn_sample_kernel(q_ref, kc_ref, vc_ref, kn_ref, vn_ref, tri_ref, trin_ref, o_ref, *, tk):
    ls = q_ref.shape[0]
    past, nh, hd = kc_ref.shape[2:]
    heads = tuple(range(nh))
    lanes = [slice(h * hd, (h + 1) * hd) for h in heads]
    last_kb = past // tk - 1
    tri = tri_ref[...]
    trin = trin_ref[...]
    row, col = _strict_tri(ls)
    new_mask = col < row
    qs = [q_ref[:, lanes[h]] for h in heads]

    def cache_chains(kb):
        rows = pl.ds(pl.multiple_of(kb * tk, tk), tk)
        return [(h, qs[h], kc_ref[0, 0, rows, h, :].astype(BF16), vc_ref[0, 0, rows, h, :].astype(BF16), tri, None)
                for h in heads]

    new_chains = [(h, qs[h], kn_ref[:, lanes[h]], vn_ref[:, lanes[h]], trin, new_mask) for h in heads]
    carry = [jnp.zeros((ls, 1), F32) for _ in heads]
    acc = [jnp.zeros((ls, hd), F32) for _ in heads]
    carry, acc = _sb_chains(new_chains + cache_chains(last_kb), carry, acc)
    acc = _sb_older_blocks(cache_chains, last_kb - 1, carry, acc)
    for h in heads:
        o_ref[:, lanes[h]] = acc[h].astype(BF16)


def _sb_attn_sample(q, k_cache, v_cache, layer, k_new, v_new, ls):
    t, d = q.shape
    b = t // ls
    _, cb, past, nh, hd = k_cache.shape
    tk = min(SB_BLOCK, past)
    hp = min(SB_SAMPLE_LANES, d)
    hps = hp // hd
    assert cb == b and nh * hd == d and past % tk == 0 and d % hp == 0 and hp % LANES == 0 and hps % SUBLANES == 0
    new = pl.BlockSpec((ls, hp), lambda bi, hi: (bi, hi))
    old = pl.BlockSpec((1, 1, past, hps, hd), lambda bi, hi: (layer, bi, 0, hi, 0))
    return pl.pallas_call(
        functools.partial(_sb_attn_sample_kernel, tk=tk),
        grid=(b, d // hp),
        in_specs=[
            new, old, old, new, new,
            _resident((tk, tk)),
            _resident((ls, ls)),
        ],
        out_specs=new,
        out_shape=jax.ShapeDtypeStruct((t, d), BF16),
        compiler_params=_params("parallel", "parallel"),
        name="sb_attn_sample",
    )(q, k_cache, v_cache, k_new, v_new, _tri_matrix(tk), _tri_matrix(ls))


def _rotary_tables(pos, half, reps):
    freqs = ROPE_BASE ** (-jnp.arange(half, dtype=F32) / half)
    ang = pos.astype(F32)[:, None] * freqs[None, :]
    cos, sin = jnp.cos(ang), jnp.sin(ang)
    cos = jnp.concatenate([cos, cos], axis=-1)
    sin = jnp.concatenate([-sin, sin], axis=-1)
    return jnp.tile(cos, (reps, 1)), jnp.tile(sin, (reps, 1))


def kernel(x_prompt, x_sample, p_prompt, p_sample, state_ret, cache_sb_k, cache_sb_v, norm_ffn_a, w_ffn_a_gu, w_ffn_a_down, norm_mix, ret_w_in, ret_w_out, sb_w_qkv, sb_w_out, norm_ffn_b, w_ffn_b_gu, w_ffn_b_down, norm_ple, ple_w_gate, ple_w_proj, norm_final):
    bp, sp, d = x_prompt.shape
    bs, ls, _ = x_sample.shape
    depth = norm_ffn_a.shape[0]
    past = cache_sb_k.shape[2]
    dk_w = d // H_RET
    dv_w = ret_w_out.shape[1] // H_RET
    hd = d // H_SB

    hp = x_prompt.reshape(bp * sp, d)
    hs = x_sample.reshape(bs * ls, d)
    pp = p_prompt.reshape(depth, bp * sp, -1)
    ps = p_sample.reshape(depth, bs * ls, -1)
    row = lambda g: g.reshape(1, d)

    cos_p, sin_p = _rotary_tables(jnp.arange(sp), dk_w // 2, 1)
    cos_s, sin_s = _rotary_tables(past + jnp.arange(ls), dk_w // 2, bs)
    ret_chunk_p = min(RET_BLOCK, sp)
    ret_chunk_s = min(RET_CHUNK, ls)
    zero_state = jnp.zeros((bp, H_RET, dk_w, dv_w), F32)
    g_final = row(norm_final)

    n_sb = depth // 2
    kp_all = jnp.zeros((n_sb, bp * sp, d), F32)
    vp_all = jnp.zeros((n_sb, bp * sp, d), F32)
    ks_all = jnp.zeros((n_sb, bs * ls, d), F32)
    vs_all = jnp.zeros((n_sb, bs * ls, d), F32)

    ret_p, ret_s = [], []
    for i in range(depth):
        j = i // 2
        wa_gu, wa_dn = w_ffn_a_gu[i].astype(BF16), w_ffn_a_down[i].astype(BF16)
        hp = _ffn(hp, row(norm_ffn_a[i]), wa_gu, wa_dn)
        hs = _ffn(hs, row(norm_ffn_a[i]), wa_gu, wa_dn)
        g_mix = row(norm_mix[i])
        if i % 2 == 0:
            w_in, w_out = ret_w_in[j].astype(BF16), ret_w_out[j].astype(BF16)
            q, k, v, gate = _ret_proj(hp, g_mix, w_in, cos_p, sin_p)
            mix_p, s_new = _ret_scan(q, k, v, gate, zero_state, sp, ret_chunk_p)
            ret_p.append(s_new)
            q, k, v, gate = _ret_proj(hs, g_mix, w_in, cos_s, sin_s)
            mix_s, s_new = _ret_scan(q, k, v, gate, state_ret[j], ls, ret_chunk_s)
            ret_s.append(s_new)
        else:
            w_qkv, w_out = sb_w_qkv[j].astype(BF16), sb_w_out[j].astype(BF16)
            q, kp_all, vp_all, kb, vb = _sb_proj(hp, g_mix, w_qkv, kp_all, vp_all, j)
            mix_p = _sb_attn(q, kb, vb, sp)
            q, ks_all, vs_all, kb, vb = _sb_proj(hs, g_mix, w_qkv, ks_all, vs_all, j)
            mix_s = _sb_attn_sample(q, cache_sb_k, cache_sb_v, j, kb, vb, ls)
        post = (w_out, row(norm_ffn_b[i]), w_ffn_b_gu[i].astype(BF16), w_ffn_b_down[i].astype(BF16),
                row(norm_ple[i]), ple_w_gate[i].astype(BF16), ple_w_proj[i].astype(BF16), g_final, i == depth - 1)
        hp = _post_mixer(mix_p, hp, pp, i, *post)
        hs = _post_mixer(mix_s, hs, ps, i, *post)

    return (hp.reshape(bp, sp, d), hs.reshape(bs, ls, d),
            jnp.stack(ret_p), jnp.stack(ret_s),
            kp_all.reshape(n_sb, bp, sp, H_SB, hd), vp_all.reshape(n_sb, bp, sp, H_SB, hd),
            ks_all.reshape(n_sb, bs, ls, H_SB, hd), vs_all.reshape(n_sb, bs, ls, H_SB, hd))
```

```python
import functools

import jax
import jax.numpy as jnp
from jax import lax
from jax.experimental import pallas as pl
from jax.experimental.pallas import tpu as pltpu

F32 = jnp.float32
BF16 = jnp.bfloat16

H_RET = 8
H_SB = 16
RET_CHUNK = 64
ROPE_BASE = 10000.0
EPS = 1e-6
GN_EPS = 1e-5

LANES = 128
VMEM_LIMIT_BYTES = 56 * 1024 * 1024
TOKEN_TILE = 512
FFN_COLS = 1024
RET_BLOCK = 256
SB_BLOCK = 256
SB_PROMPT_LANES = 256
SB_SAMPLE_LANES = 512
SB_SKIP_MASS = 104.0


def _params(*semantics):
    return pltpu.CompilerParams(dimension_semantics=semantics, vmem_limit_bytes=VMEM_LIMIT_BYTES)


def _rmsnorm(x, g):
    return x * lax.rsqrt(jnp.mean(x * x, axis=-1, keepdims=True) + EPS) * g


def _dot(a, b):
    return jnp.dot(a, b, preferred_element_type=F32)


def _dot_nt(a, b):
    return lax.dot_general(a, b, (((1,), (1,)), ((), ())), preferred_element_type=F32)


def _dot_tn(a, b):
    return lax.dot_general(a, b, (((0,), (0,)), ((), ())), preferred_element_type=F32)


def _ffn_body(x, g_ref, wgu_ref, wd_ref, acc_ref):
    d_ff = wd_ref.shape[0]
    xn = _rmsnorm(x, g_ref[...]).astype(BF16)
    for c0 in range(0, d_ff, FFN_COLS):
        cw = min(FFN_COLS, d_ff - c0)
        a = _dot(xn, wgu_ref[:, c0:c0 + cw])
        b = _dot(xn, wgu_ref[:, d_ff + c0:d_ff + c0 + cw])
        h = (a * jax.nn.sigmoid(a) * b).astype(BF16)
        part = _dot(h, wd_ref[c0:c0 + cw, :])
        if c0 == 0:
            acc_ref[...] = part
        else:
            acc_ref[...] += part
    return x + 0.5 * acc_ref[...]


def _ffn_kernel(x_ref, g_ref, wgu_ref, wd_ref, o_ref, acc_ref):
    o_ref[...] = _ffn_body(x_ref[...], g_ref, wgu_ref, wd_ref, acc_ref)


def _resident(shape):
    return pl.BlockSpec(shape, lambda *_: (0,) * len(shape), pipeline_mode=pl.Buffered(1))


def _ffn(x, g, w_gu, w_down):
    t, d = x.shape
    d_ff = w_down.shape[0]
    tm = min(TOKEN_TILE, t)
    assert t % tm == 0 and w_gu.shape == (d, 2 * d_ff) and d_ff % LANES == 0
    return pl.pallas_call(
        _ffn_kernel,
        grid=(t // tm,),
        in_specs=[
            pl.BlockSpec((tm, d), lambda i: (i, 0)),
            _resident((1, d)),
            _resident((d, 2 * d_ff)),
            _resident((d_ff, d)),
        ],
        out_specs=pl.BlockSpec((tm, d), lambda i: (i, 0)),
        out_shape=jax.ShapeDtypeStruct((t, d), F32),
        scratch_shapes=[pltpu.VMEM((tm, d), F32)],
        compiler_params=_params("parallel"),
        name="ffn",
    )(x, g, w_gu, w_down)


def _post_mixer_kernel(a_ref, x_ref, p_ref, wo_ref, gb_ref, wgu_ref, wd_ref, gp_ref, wg_ref, wp_ref, gf_ref,
                       o_ref, acc_ref, *, final):
    x = x_ref[...] + _dot(a_ref[...], wo_ref[...])
    x = _ffn_body(x, gb_ref, wgu_ref, wd_ref, acc_ref)
    xn = _rmsnorm(x, gp_ref[...]).astype(BF16)
    gate = jax.nn.sigmoid(_dot(xn, wg_ref[...]))
    y = x + gate * _dot(p_ref[0].astype(BF16), wp_ref[...])
    if final:
        y = _rmsnorm(y, gf_ref[...])
    o_ref[...] = y


def _post_mixer(a, x, p_all, layer, w_out, g_ffn, w_gu, w_down, g_ple, w_gate, w_proj, g_final, final):
    t, d = x.shape
    kdim = a.shape[1]
    d_ff = w_down.shape[0]
    pd = p_all.shape[2]
    tm = min(TOKEN_TILE, t)
    assert t % tm == 0 and w_gu.shape == (d, 2 * d_ff) and d_ff % LANES == 0
    return pl.pallas_call(
        functools.partial(_post_mixer_kernel, final=final),
        grid=(t // tm,),
        in_specs=[
            pl.BlockSpec((tm, kdim), lambda i: (i, 0)),
            pl.BlockSpec((tm, d), lambda i: (i, 0)),
            pl.BlockSpec((1, tm, pd), lambda i: (layer, i, 0)),
            _resident((kdim, d)),
            _resident((1, d)),
            _resident((d, 2 * d_ff)),
            _resident((d_ff, d)),
            _resident((1, d)),
            _resident((d, d)),
            _resident((pd, d)),
            _resident((1, d)),
        ],
        out_specs=pl.BlockSpec((tm, d), lambda i: (i, 0)),
        out_shape=jax.ShapeDtypeStruct((t, d), F32),
        scratch_shapes=[pltpu.VMEM((tm, d), F32)],
        compiler_params=_params("parallel"),
        name="post_mixer",
    )(a, x, p_all, w_out, g_ffn, w_gu, w_down, g_ple, w_gate, w_proj, g_final)


def _ret_proj_kernel(x_ref, g_ref, w_ref, cos_ref, sin_ref, q_ref, k_ref, v_ref, gate_ref):
    d = x_ref.shape[1]
    xn = _rmsnorm(x_ref[...], g_ref[...]).astype(BF16)
    cos = cos_ref[...]
    sin = sin_ref[...]
    for part, out in ((0, q_ref), (1, k_ref)):
        r = _dot(xn, w_ref[:, part * d:(part + 1) * d])
        for h in range(d // LANES):
            rh = r[:, h * LANES:(h + 1) * LANES]
            out[:, h * LANES:(h + 1) * LANES] = (rh * cos + pltpu.roll(rh, LANES // 2, 1) * sin).astype(BF16)
    for c in range(2):
        v_ref[:, c * d:(c + 1) * d] = _dot(xn, w_ref[:, (2 + c) * d:(3 + c) * d]).astype(BF16)
        gate_ref[:, c * d:(c + 1) * d] = _dot(xn, w_ref[:, (4 + c) * d:(5 + c) * d])


def _ret_proj(x, g, w_in, cos, sin):
    t, d = x.shape
    tm = min(TOKEN_TILE, t)
    nrot = cos.shape[0] // tm
    assert t % tm == 0 and cos.shape[0] % tm == 0 and w_in.shape[1] == 6 * d
    return pl.pallas_call(
        _ret_proj_kernel,
        grid=(t // tm,),
        in_specs=[
            pl.BlockSpec((tm, d), lambda i: (i, 0)),
            _resident((1, d)),
            _resident((d, 6 * d)),
            pl.BlockSpec((tm, LANES), lambda i: (i % nrot, 0)),
            pl.BlockSpec((tm, LANES), lambda i: (i % nrot, 0)),
        ],
        out_specs=[
            pl.BlockSpec((tm, d), lambda i: (i, 0)),
            pl.BlockSpec((tm, d), lambda i: (i, 0)),
            pl.BlockSpec((tm, 2 * d), lambda i: (i, 0)),
            pl.BlockSpec((tm, 2 * d), lambda i: (i, 0)),
        ],
        out_shape=[
            jax.ShapeDtypeStruct((t, d), BF16),
            jax.ShapeDtypeStruct((t, d), BF16),
            jax.ShapeDtypeStruct((t, 2 * d), BF16),
            jax.ShapeDtypeStruct((t, 2 * d), F32),
        ],
        compiler_params=_params("parallel"),
        name="ret_proj",
    )(x, g, w_in, cos, sin)


def _ret_scan_kernel(q_ref, k_ref, v_ref, g_ref, s0_ref, din_ref, dq_ref, dk_ref, dc_ref,
                     y_ref, so_ref, s_ref):
    c = pl.program_id(1)
    dk_w = q_ref.shape[1] // H_RET
    dv_w = v_ref.shape[1] // H_RET

    @pl.when(c == 0)
    def _():
        s_ref[...] = s0_ref[0]

    heads = range(H_RET)
    qs = [q_ref[:, h * dk_w:(h + 1) * dk_w] for h in heads]
    ks = [k_ref[:, h * dk_w:(h + 1) * dk_w] for h in heads]
    vs = [v_ref[:, h * dv_w:(h + 1) * dv_w] for h in heads]
    ss = [s_ref[h] for h in heads]
    scores = [_dot_nt(qs[h], ks[h]) for h in heads]
    inter = [_dot(qs[h], ss[h].astype(BF16)) for h in heads]
    k_decs = [(ks[h].astype(F32) * dk_ref[h]).astype(BF16) for h in heads]
    updates = [_dot_tn(k_decs[h], vs[h]) for h in heads]
    for h in heads:
        s_ref[h] = dc_ref[h] * ss[h] + updates[h]
    probs = [(scores[h] * din_ref[h]).astype(BF16) for h in heads]
    outs = [_dot(probs[h], vs[h]) + inter[h] * dq_ref[h] for h in heads]
    for h in heads:
        o = outs[h]
        mu = jnp.mean(o, axis=-1, keepdims=True)
        oc = o - mu
        var = jnp.mean(oc * oc, axis=-1, keepdims=True)
        on = oc * lax.rsqrt(var + GN_EPS)
        gt = g_ref[:, h * dv_w:(h + 1) * dv_w]
        y_ref[:, h * dv_w:(h + 1) * dv_w] = (gt * jax.nn.sigmoid(gt) * on).astype(BF16)

    @pl.when(c == pl.num_programs(1) - 1)
    def _():
        so_ref[0] = s_ref[...]


def _ret_tables(c, dk_w, dv_w):
    scale = dk_w ** -0.5
    log_g = jnp.log1p(-(2.0 ** (-5.0 - jnp.arange(H_RET, dtype=F32))))
    idx = jnp.arange(c, dtype=F32)
    rel = idx[:, None] - idx[None, :]
    din = jnp.where(rel[None] >= 0, jnp.exp(jnp.maximum(rel, 0.0)[None] * log_g[:, None, None]), 0.0) * scale
    dq = jnp.exp((idx + 1.0)[None, :] * log_g[:, None])
    dk = jnp.exp((c - 1.0 - idx)[None, :] * log_g[:, None]) * scale
    dc = jnp.exp(c * log_g)
    dq = jnp.broadcast_to(dq[:, :, None], (H_RET, c, dv_w))
    dk = jnp.broadcast_to(dk[:, :, None], (H_RET, c, dk_w))
    dc = jnp.broadcast_to(dc[:, None, None], (H_RET, 1, dv_w))
    return din, dq, dk, dc


def _ret_scan(q, k, v, gate, s0, seq_len, chunk):
    t, dqk = q.shape
    dvv = v.shape[1]
    b = t // seq_len
    nc = seq_len // chunk
    dk_w, dv_w = dqk // H_RET, dvv // H_RET
    assert nc * chunk == seq_len and s0.shape == (b, H_RET, dk_w, dv_w)
    din, dq, dk, dc = _ret_tables(chunk, dk_w, dv_w)
    tok = lambda bi, ci: (bi * nc + ci, 0)
    state = lambda bi, ci: (bi, 0, 0, 0)
    return pl.pallas_call(
        _ret_scan_kernel,
        grid=(b, nc),
        in_specs=[
            pl.BlockSpec((chunk, dqk), tok),
            pl.BlockSpec((chunk, dqk), tok),
            pl.BlockSpec((chunk, dvv), tok),
            pl.BlockSpec((chunk, dvv), tok),
            pl.BlockSpec((1, H_RET, dk_w, dv_w), state),
            _resident((H_RET, chunk, chunk)),
            _resident((H_RET, chunk, dv_w)),
            _resident((H_RET, chunk, dk_w)),
            _resident((H_RET, 1, dv_w)),
        ],
        out_specs=[
            pl.BlockSpec((chunk, dvv), tok),
            pl.BlockSpec((1, H_RET, dk_w, dv_w), state),
        ],
        out_shape=[
            jax.ShapeDtypeStruct((t, dvv), BF16),
            jax.ShapeDtypeStruct((b, H_RET, dk_w, dv_w), F32),
        ],
        scratch_shapes=[pltpu.VMEM((H_RET, dk_w, dv_w), F32)],
        compiler_params=_params("parallel", "arbitrary"),
        name="ret_scan",
    )(q, k, v, gate, s0, din, dq, dk, dc)


def _sb_proj_kernel(x_ref, g_ref, w_ref, k_all_ref, v_all_ref, q_ref, kf_ref, vf_ref, kb_ref, vb_ref, *,
                    scale, keys_on_lanes):
    del k_all_ref, v_all_ref
    d = x_ref.shape[1]
    xn = _rmsnorm(x_ref[...], g_ref[...]).astype(BF16)
    q_ref[...] = (_dot(xn, w_ref[:, 0:d]) * scale).astype(BF16)
    k = _dot(xn, w_ref[:, d:2 * d])
    kb_ref[...] = k.astype(BF16)
    v = _dot(xn, w_ref[:, 2 * d:3 * d])
    vb_ref[...] = v.astype(BF16)
    if keys_on_lanes:
        kf_ref[0, 0] = k.T
        vf_ref[0, 0] = v.T
    else:
        kf_ref[0] = k
        vf_ref[0] = v


def _sb_proj(x, g, w_qkv, k_all, v_all, layer, seq_len=None):
    t, d = x.shape
    tm = min(TOKEN_TILE, t)
    assert t % tm == 0 and w_qkv.shape[1] == 3 * d
    scale = (d // H_SB) ** -0.5
    tok = pl.BlockSpec((tm, d), lambda i: (i, 0))
    if seq_len is None:
        assert k_all.shape[1:] == (t, d)
        slab = pl.BlockSpec((1, tm, d), lambda i: (layer, i, 0))
    else:
        tiles = seq_len // tm
        assert tiles * tm == seq_len and k_all.shape[1:] == (t // seq_len, d, seq_len)
        slab = pl.BlockSpec((1, 1, d, tm), lambda i: (layer, i // tiles, 0, i % tiles))
    hbm = pl.BlockSpec(memory_space=pl.ANY)
    return pl.pallas_call(
        functools.partial(_sb_proj_kernel, scale=scale, keys_on_lanes=seq_len is not None),
        grid=(t // tm,),
        in_specs=[
            tok,
            _resident((1, d)),
            _resident((d, 3 * d)),
            hbm,
            hbm,
        ],
        out_specs=[tok, slab, slab, tok, tok],
        out_shape=[
            jax.ShapeDtypeStruct((t, d), BF16),
            jax.ShapeDtypeStruct(k_all.shape, F32),
            jax.ShapeDtypeStruct(v_all.shape, F32),
            jax.ShapeDtypeStruct((t, d), BF16),
            jax.ShapeDtypeStruct((t, d), BF16),
        ],
        input_output_aliases={3: 1, 4: 2},
        compiler_params=_params("parallel"),
        name="sb_proj",
    )(x, g, w_qkv, k_all, v_all)


def _softplus(z):
    return jnp.maximum(z, 0.0) + jnp.log(1.0 + jnp.exp(-jnp.abs(z)))


def _sb_chains(chains, carry, acc):
    carry, acc = list(carry), list(acc)
    zs = [_dot(q, k) if keys_on_lanes else _dot_nt(q, k) for _, q, k, _, _, _, keys_on_lanes in chains]
    sps = []
    for z, (_, _, _, _, _, mask, _) in zip(zs, chains):
        sp = _softplus(z)
        sps.append(sp if mask is None else jnp.where(mask, sp, 0.0))
    newers = []
    for sp, (_, _, _, _, tri, _, _) in zip(sps, chains):
        hi = sp.astype(BF16)
        lo = (sp - hi.astype(F32)).astype(BF16)
        newers.append(_dot(hi, tri) + _dot(lo, tri))
    weights = []
    for z, sp, newer, (h, _, _, _, _, mask, _) in zip(zs, sps, newers, chains):
        a = jnp.exp(z - sp - newer - carry[h])
        weights.append((a if mask is None else jnp.where(mask, a, 0.0)).astype(BF16))
        carry[h] = carry[h] + jnp.sum(sp, axis=-1, keepdims=True)
    for a, (h, _, _, v, _, _, keys_on_lanes) in zip(weights, chains):
        acc[h] = acc[h] + (_dot_nt(a, v) if keys_on_lanes else _dot(a, v))
    return carry, acc


def _min_mass(carry):
    m = jnp.min(carry[0])
    for c in carry[1:]:
        m = jnp.minimum(m, jnp.min(c))
    return m


def _sb_older_blocks(chains_of, first_kb, carry, acc):
    n = len(carry)

    def cond(state):
        return jnp.logical_and(state[0] >= 0, state[1] < SB_SKIP_MASS)

    def body(state):
        kb = state[0]
        c, a = _sb_chains(chains_of(kb), state[2:2 + n], state[2 + n:])
        return (kb - 1, _min_mass(c), *c, *a)

    state = lax.while_loop(cond, body, (first_kb, _min_mass(carry), *carry, *acc))
    return state[2 + n:]


def _strict_tri(n):
    r = lax.broadcasted_iota(jnp.int32, (n, n), 0)
    c = lax.broadcasted_iota(jnp.int32, (n, n), 1)
    return r, c


def _sb_attn_kernel(q_ref, k_ref, v_ref, tri_ref, o_ref, *, hd):
    qi = pl.program_id(2)
    tq = q_ref.shape[0]
    heads = tuple(range(q_ref.shape[1] // hd))
    lanes = [slice(h * hd, (h + 1) * hd) for h in heads]
    tri = tri_ref[...]
    row, col = _strict_tri(tq)
    diag_mask = col < row
    qs = [q_ref[:, lanes[h]] for h in heads]

    def chains_of(kb, mask=None):
        ks = pl.multiple_of(kb * tq, tq)
        return [(h, qs[h], k_ref[pl.ds(ks, tq), lanes[h]], v_ref[pl.ds(ks, tq), lanes[h]], tri, mask, False)
                for h in heads]

    carry0 = [jnp.zeros((tq, 1), F32) for _ in heads]
    acc0 = [jnp.zeros((tq, hd), F32) for _ in heads]

    def store(acc):
        for h in heads:
            o_ref[:, lanes[h]] = acc[h].astype(BF16)

    @pl.when(qi == 0)
    def _():
        store(_sb_chains(chains_of(0, diag_mask), carry0, acc0)[1])

    @pl.when(qi > 0)
    def _():
        carry, acc = _sb_chains(chains_of(qi, diag_mask) + chains_of(qi - 1), carry0, acc0)
        store(_sb_older_blocks(chains_of, qi - 2, carry, acc))


def _tri_matrix(n):
    r, c = _strict_tri(n)
    return (r > c).astype(BF16)


def _sb_attn(q, k, v, seq_len):
    t, d = q.shape
    b = t // seq_len
    tq = min(SB_BLOCK, seq_len)
    nq = seq_len // tq
    hp = min(SB_PROMPT_LANES, d)
    assert nq * tq == seq_len and d % hp == 0 and hp % LANES == 0
    qmap = lambda bi, hi, qi: (bi * nq + qi, hi)
    kvmap = lambda bi, hi, qi: (bi, hi)
    return pl.pallas_call(
        functools.partial(_sb_attn_kernel, hd=d // H_SB),
        grid=(b, d // hp, nq),
        in_specs=[
            pl.BlockSpec((tq, hp), qmap),
            pl.BlockSpec((seq_len, hp), kvmap),
            pl.BlockSpec((seq_len, hp), kvmap),
            _resident((tq, tq)),
        ],
        out_specs=pl.BlockSpec((tq, hp), qmap),
        out_shape=jax.ShapeDtypeStruct((t, d), BF16),
        compiler_params=_params("parallel", "parallel", "arbitrary"),
        name="sb_attn",
    )(q, k, v, _tri_matrix(tq))


def _sb_attn_sample_kernel(q_ref, kc_ref, vc_ref, kn_ref, vn_ref, tri_ref, trin_ref, o_ref, *, tk, hd):
    ls = q_ref.shape[0]
    past = kc_ref.shape[3]
    heads = tuple(range(q_ref.shape[1] // hd))
    lanes = [slice(h * hd, (h + 1) * hd) for h in heads]
    last_kb = past // tk - 1
    tri = tri_ref[...]
    trin = trin_ref[...]
    row, col = _strict_tri(ls)
    new_mask = col < row
    qs = [q_ref[:, lanes[h]] for h in heads]

    def cache_chains(kb):
        keys = pl.ds(pl.multiple_of(kb * tk, tk), tk)
        return [(h, qs[h], kc_ref[0, 0, lanes[h], keys].astype(BF16), vc_ref[0, 0, lanes[h], keys].astype(BF16),
                 tri, None, True) for h in heads]

    new_chains = [(h, qs[h], kn_ref[:, lanes[h]], vn_ref[:, lanes[h]], trin, new_mask, False) for h in heads]
    carry = [jnp.zeros((ls, 1), F32) for _ in heads]
    acc = [jnp.zeros((ls, hd), F32) for _ in heads]
    carry, acc = _sb_chains(new_chains + cache_chains(last_kb), carry, acc)
    acc = _sb_older_blocks(cache_chains, last_kb - 1, carry, acc)
    for h in heads:
        o_ref[:, lanes[h]] = acc[h].astype(BF16)


def _sb_attn_sample(q, k_cache, v_cache, layer, k_new, v_new, ls):
    t, d = q.shape
    b = t // ls
    _, cb, cd, past = k_cache.shape
    tk = min(SB_BLOCK, past)
    hp = min(SB_SAMPLE_LANES, d)
    assert cb == b and cd == d and past % tk == 0 and tk % LANES == 0 and d % hp == 0 and hp % LANES == 0
    new = pl.BlockSpec((ls, hp), lambda bi, hi: (bi, hi))
    old = pl.BlockSpec((1, 1, hp, past), lambda bi, hi: (layer, bi, hi, 0))
    return pl.pallas_call(
        functools.partial(_sb_attn_sample_kernel, tk=tk, hd=d // H_SB),
        grid=(b, d // hp),
        in_specs=[
            new, old, old, new, new,
            _resident((tk, tk)),
            _resident((ls, ls)),
        ],
        out_specs=new,
        out_shape=jax.ShapeDtypeStruct((t, d), BF16),
        compiler_params=_params("parallel", "parallel"),
        name="sb_attn_sample",
    )(q, k_cache, v_cache, k_new, v_new, _tri_matrix(tk), _tri_matrix(ls))


def _rotary_tables(pos, half, reps):
    freqs = ROPE_BASE ** (-jnp.arange(half, dtype=F32) / half)
    ang = pos.astype(F32)[:, None] * freqs[None, :]
    cos, sin = jnp.cos(ang), jnp.sin(ang)
    cos = jnp.concatenate([cos, cos], axis=-1)
    sin = jnp.concatenate([-sin, sin], axis=-1)
    return jnp.tile(cos, (reps, 1)), jnp.tile(sin, (reps, 1))


def kernel(x_prompt, x_sample, p_prompt, p_sample, state_ret, cache_sb_k, cache_sb_v, norm_ffn_a, w_ffn_a_gu, w_ffn_a_down, norm_mix, ret_w_in, ret_w_out, sb_w_qkv, sb_w_out, norm_ffn_b, w_ffn_b_gu, w_ffn_b_down, norm_ple, ple_w_gate, ple_w_proj, norm_final):
    bp, sp, d = x_prompt.shape
    bs, ls, _ = x_sample.shape
    depth = norm_ffn_a.shape[0]
    past = cache_sb_k.shape[2]
    dk_w = d // H_RET
    dv_w = ret_w_out.shape[1] // H_RET
    hd = d // H_SB

    hp = x_prompt.reshape(bp * sp, d)
    hs = x_sample.reshape(bs * ls, d)
    pp = p_prompt.reshape(depth, bp * sp, -1)
    ps = p_sample.reshape(depth, bs * ls, -1)
    row = lambda g: g.reshape(1, d)

    cos_p, sin_p = _rotary_tables(jnp.arange(sp), dk_w // 2, 1)
    cos_s, sin_s = _rotary_tables(past + jnp.arange(ls), dk_w // 2, bs)
    ret_chunk_p = min(RET_BLOCK, sp)
    ret_chunk_s = min(RET_CHUNK, ls)
    zero_state = jnp.zeros((bp, H_RET, dk_w, dv_w), F32)
    g_final = row(norm_final)

    n_sb = depth // 2
    kp_all = jnp.zeros((n_sb, bp, d, sp), F32)
    vp_all = jnp.zeros((n_sb, bp, d, sp), F32)
    ks_all = jnp.zeros((n_sb, bs * ls, d), F32)
    vs_all = jnp.zeros((n_sb, bs * ls, d), F32)
    keys_on_lanes = lambda c: jnp.transpose(c, (0, 1, 3, 4, 2)).reshape(c.shape[0], c.shape[1], d, c.shape[2])
    positions_first = lambda c: jnp.transpose(c.reshape(n_sb, bp, H_SB, hd, sp), (0, 1, 4, 2, 3))
    cache_k, cache_v = keys_on_lanes(cache_sb_k), keys_on_lanes(cache_sb_v)

    ret_p, ret_s = [], []
    for i in range(depth):
        j = i // 2
        wa_gu, wa_dn = w_ffn_a_gu[i].astype(BF16), w_ffn_a_down[i].astype(BF16)
        hp = _ffn(hp, row(norm_ffn_a[i]), wa_gu, wa_dn)
        hs = _ffn(hs, row(norm_ffn_a[i]), wa_gu, wa_dn)
        g_mix = row(norm_mix[i])
        if i % 2 == 0:
            w_in, w_out = ret_w_in[j].astype(BF16), ret_w_out[j].astype(BF16)
            q, k, v, gate = _ret_proj(hp, g_mix, w_in, cos_p, sin_p)
            mix_p, s_new = _ret_scan(q, k, v, gate, zero_state, sp, ret_chunk_p)
            ret_p.append(s_new)
            q, k, v, gate = _ret_proj(hs, g_mix, w_in, cos_s, sin_s)
            mix_s, s_new = _ret_scan(q, k, v, gate, state_ret[j], ls, ret_chunk_s)
            ret_s.append(s_new)
        else:
            w_qkv, w_out = sb_w_qkv[j].astype(BF16), sb_w_out[j].astype(BF16)
            q, kp_all, vp_all, kb, vb = _sb_proj(hp, g_mix, w_qkv, kp_all, vp_all, j, seq_len=sp)
            mix_p = _sb_attn(q, kb, vb, sp)
            q, ks_all, vs_all, kb, vb = _sb_proj(hs, g_mix, w_qkv, ks_all, vs_all, j)
            mix_s = _sb_attn_sample(q, cache_k, cache_v, j, kb, vb, ls)
        post = (w_out, row(norm_ffn_b[i]), w_ffn_b_gu[i].astype(BF16), w_ffn_b_down[i].astype(BF16),
                row(norm_ple[i]), ple_w_gate[i].astype(BF16), ple_w_proj[i].astype(BF16), g_final, i == depth - 1)
        hp = _post_mixer(mix_p, hp, pp, i, *post)
        hs = _post_mixer(mix_s, hs, ps, i, *post)

    return (hp.reshape(bp, sp, d), hs.reshape(bs, ls, d),
            jnp.stack(ret_p), jnp.stack(ret_s),
            positions_first(kp_all), positions_first(vp_all),
            ks_all.reshape(n_sb, bs, ls, H_SB, hd), vs_all.reshape(n_sb, bs, ls, H_SB, hd))
```

```python
import functools

import jax
import jax.numpy as jnp
from jax import lax
from jax.experimental import pallas as pl
from jax.experimental.pallas import tpu as pltpu

F32 = jnp.float32
BF16 = jnp.bfloat16

H_RET = 8
H_SB = 16
RET_CHUNK = 64
ROPE_BASE = 10000.0
EPS = 1e-6
GN_EPS = 1e-5

LANES = 128
VMEM_LIMIT_BYTES = 56 * 1024 * 1024
TOKEN_TILE = 512
FFN_COLS = 1024
RET_BLOCK = 256
SB_BLOCK = 256
SB_PROMPT_LANES = 512
SB_SAMPLE_LANES = 512
SB_SKIP_MASS = 104.0


def _params(*semantics):
    return pltpu.CompilerParams(dimension_semantics=semantics, vmem_limit_bytes=VMEM_LIMIT_BYTES)


def _rmsnorm(x, g):
    return x * lax.rsqrt(jnp.mean(x * x, axis=-1, keepdims=True) + EPS) * g


def _dot(a, b):
    return jnp.dot(a, b, preferred_element_type=F32)


def _dot_nt(a, b):
    return lax.dot_general(a, b, (((1,), (1,)), ((), ())), preferred_element_type=F32)


def _dot_tn(a, b):
    return lax.dot_general(a, b, (((0,), (0,)), ((), ())), preferred_element_type=F32)


def _ffn_body(x, g_ref, wgu_ref, wd_ref, acc_ref):
    d_ff = wd_ref.shape[0]
    xn = _rmsnorm(x, g_ref[...]).astype(BF16)
    for c0 in range(0, d_ff, FFN_COLS):
        cw = min(FFN_COLS, d_ff - c0)
        a = _dot(xn, wgu_ref[:, c0:c0 + cw])
        b = _dot(xn, wgu_ref[:, d_ff + c0:d_ff + c0 + cw])
        h = (a * jax.nn.sigmoid(a) * b).astype(BF16)
        part = _dot(h, wd_ref[c0:c0 + cw, :])
        if c0 == 0:
            acc_ref[...] = part
        else:
            acc_ref[...] += part
    return x + 0.5 * acc_ref[...]


def _ffn_kernel(x_ref, g_ref, wgu_ref, wd_ref, o_ref, acc_ref):
    o_ref[...] = _ffn_body(x_ref[...], g_ref, wgu_ref, wd_ref, acc_ref)


def _resident(shape):
    return pl.BlockSpec(shape, lambda *_: (0,) * len(shape), pipeline_mode=pl.Buffered(1))


def _ffn(x, g, w_gu, w_down):
    t, d = x.shape
    d_ff = w_down.shape[0]
    tm = min(TOKEN_TILE, t)
    assert t % tm == 0 and w_gu.shape == (d, 2 * d_ff) and d_ff % LANES == 0
    return pl.pallas_call(
        _ffn_kernel,
        grid=(t // tm,),
        in_specs=[
            pl.BlockSpec((tm, d), lambda i: (i, 0)),
            _resident((1, d)),
            _resident((d, 2 * d_ff)),
            _resident((d_ff, d)),
        ],
        out_specs=pl.BlockSpec((tm, d), lambda i: (i, 0)),
        out_shape=jax.ShapeDtypeStruct((t, d), F32),
        scratch_shapes=[pltpu.VMEM((tm, d), F32)],
        compiler_params=_params("parallel"),
        name="ffn",
    )(x, g, w_gu, w_down)


def _post_mixer_kernel(a_ref, x_ref, p_ref, wo_ref, gb_ref, wgu_ref, wd_ref, gp_ref, wg_ref, wp_ref, gf_ref,
                       o_ref, acc_ref, *, final):
    x = x_ref[...] + _dot(a_ref[...], wo_ref[...])
    x = _ffn_body(x, gb_ref, wgu_ref, wd_ref, acc_ref)
    xn = _rmsnorm(x, gp_ref[...]).astype(BF16)
    gate = jax.nn.sigmoid(_dot(xn, wg_ref[...]))
    y = x + gate * _dot(p_ref[0].astype(BF16), wp_ref[...])
    if final:
        y = _rmsnorm(y, gf_ref[...])
    o_ref[...] = y


def _post_mixer(a, x, p_all, layer, w_out, g_ffn, w_gu, w_down, g_ple, w_gate, w_proj, g_final, final):
    t, d = x.shape
    kdim = a.shape[1]
    d_ff = w_down.shape[0]
    pd = p_all.shape[2]
    tm = min(TOKEN_TILE, t)
    assert t % tm == 0 and w_gu.shape == (d, 2 * d_ff) and d_ff % LANES == 0
    return pl.pallas_call(
        functools.partial(_post_mixer_kernel, final=final),
        grid=(t // tm,),
        in_specs=[
            pl.BlockSpec((tm, kdim), lambda i: (i, 0)),
            pl.BlockSpec((tm, d), lambda i: (i, 0)),
            pl.BlockSpec((1, tm, pd), lambda i: (layer, i, 0)),
            _resident((kdim, d)),
            _resident((1, d)),
            _resident((d, 2 * d_ff)),
            _resident((d_ff, d)),
            _resident((1, d)),
            _resident((d, d)),
            _resident((pd, d)),
            _resident((1, d)),
        ],
        out_specs=pl.BlockSpec((tm, d), lambda i: (i, 0)),
        out_shape=jax.ShapeDtypeStruct((t, d), F32),
        scratch_shapes=[pltpu.VMEM((tm, d), F32)],
        compiler_params=_params("parallel"),
        name="post_mixer",
    )(a, x, p_all, w_out, g_ffn, w_gu, w_down, g_ple, w_gate, w_proj, g_final)


def _ret_proj_kernel(x_ref, g_ref, w_ref, cos_ref, sin_ref, q_ref, k_ref, v_ref, gate_ref):
    d = x_ref.shape[1]
    xn = _rmsnorm(x_ref[...], g_ref[...]).astype(BF16)
    cos = cos_ref[...]
    sin = sin_ref[...]
    for part, out in ((0, q_ref), (1, k_ref)):
        r = _dot(xn, w_ref[:, part * d:(part + 1) * d])
        for h in range(d // LANES):
            rh = r[:, h * LANES:(h + 1) * LANES]
            out[:, h * LANES:(h + 1) * LANES] = (rh * cos + pltpu.roll(rh, LANES // 2, 1) * sin).astype(BF16)
    for c in range(2):
        v_ref[:, c * d:(c + 1) * d] = _dot(xn, w_ref[:, (2 + c) * d:(3 + c) * d]).astype(BF16)
        gate_ref[:, c * d:(c + 1) * d] = _dot(xn, w_ref[:, (4 + c) * d:(5 + c) * d])


def _ret_proj(x, g, w_in, cos, sin):
    t, d = x.shape
    tm = min(TOKEN_TILE, t)
    nrot = cos.shape[0] // tm
    assert t % tm == 0 and cos.shape[0] % tm == 0 and w_in.shape[1] == 6 * d
    return pl.pallas_call(
        _ret_proj_kernel,
        grid=(t // tm,),
        in_specs=[
            pl.BlockSpec((tm, d), lambda i: (i, 0)),
            _resident((1, d)),
            _resident((d, 6 * d)),
            pl.BlockSpec((tm, LANES), lambda i: (i % nrot, 0)),
            pl.BlockSpec((tm, LANES), lambda i: (i % nrot, 0)),
        ],
        out_specs=[
            pl.BlockSpec((tm, d), lambda i: (i, 0)),
            pl.BlockSpec((tm, d), lambda i: (i, 0)),
            pl.BlockSpec((tm, 2 * d), lambda i: (i, 0)),
            pl.BlockSpec((tm, 2 * d), lambda i: (i, 0)),
        ],
        out_shape=[
            jax.ShapeDtypeStruct((t, d), BF16),
            jax.ShapeDtypeStruct((t, d), BF16),
            jax.ShapeDtypeStruct((t, 2 * d), BF16),
            jax.ShapeDtypeStruct((t, 2 * d), F32),
        ],
        compiler_params=_params("parallel"),
        name="ret_proj",
    )(x, g, w_in, cos, sin)


def _ret_scan_kernel(q_ref, k_ref, v_ref, g_ref, s0_ref, din_ref, dq_ref, dk_ref, dc_ref,
                     y_ref, so_ref, s_ref):
    c = pl.program_id(1)
    dk_w = q_ref.shape[1] // H_RET
    dv_w = v_ref.shape[1] // H_RET

    @pl.when(c == 0)
    def _():
        s_ref[...] = s0_ref[0]

    heads = range(H_RET)
    qs = [q_ref[:, h * dk_w:(h + 1) * dk_w] for h in heads]
    ks = [k_ref[:, h * dk_w:(h + 1) * dk_w] for h in heads]
    vs = [v_ref[:, h * dv_w:(h + 1) * dv_w] for h in heads]
    ss = [s_ref[h] for h in heads]
    scores = [_dot_nt(qs[h], ks[h]) for h in heads]
    inter = [_dot(qs[h], ss[h].astype(BF16)) for h in heads]
    k_decs = [(ks[h].astype(F32) * dk_ref[h]).astype(BF16) for h in heads]
    updates = [_dot_tn(k_decs[h], vs[h]) for h in heads]
    for h in heads:
        s_ref[h] = dc_ref[h] * ss[h] + updates[h]
    probs = [(scores[h] * din_ref[h]).astype(BF16) for h in heads]
    outs = [_dot(probs[h], vs[h]) + inter[h] * dq_ref[h] for h in heads]
    for h in heads:
        o = outs[h]
        mu = jnp.mean(o, axis=-1, keepdims=True)
        oc = o - mu
        var = jnp.mean(oc * oc, axis=-1, keepdims=True)
        on = oc * lax.rsqrt(var + GN_EPS)
        gt = g_ref[:, h * dv_w:(h + 1) * dv_w]
        y_ref[:, h * dv_w:(h + 1) * dv_w] = (gt * jax.nn.sigmoid(gt) * on).astype(BF16)

    @pl.when(c == pl.num_programs(1) - 1)
    def _():
        so_ref[0] = s_ref[...]


def _ret_tables(c, dk_w, dv_w):
    scale = dk_w ** -0.5
    log_g = jnp.log1p(-(2.0 ** (-5.0 - jnp.arange(H_RET, dtype=F32))))
    idx = jnp.arange(c, dtype=F32)
    rel = idx[:, None] - idx[None, :]
    din = jnp.where(rel[None] >= 0, jnp.exp(jnp.maximum(rel, 0.0)[None] * log_g[:, None, None]), 0.0) * scale
    dq = jnp.exp((idx + 1.0)[None, :] * log_g[:, None])
    dk = jnp.exp((c - 1.0 - idx)[None, :] * log_g[:, None]) * scale
    dc = jnp.exp(c * log_g)
    dq = jnp.broadcast_to(dq[:, :, None], (H_RET, c, dv_w))
    dk = jnp.broadcast_to(dk[:, :, None], (H_RET, c, dk_w))
    dc = jnp.broadcast_to(dc[:, None, None], (H_RET, 1, dv_w))
    return din, dq, dk, dc


def _ret_scan(q, k, v, gate, s0, seq_len, chunk):
    t, dqk = q.shape
    dvv = v.shape[1]
    b = t // seq_len
    nc = seq_len // chunk
    dk_w, dv_w = dqk // H_RET, dvv // H_RET
    assert nc * chunk == seq_len and s0.shape == (b, H_RET, dk_w, dv_w)
    din, dq, dk, dc = _ret_tables(chunk, dk_w, dv_w)
    tok = lambda bi, ci: (bi * nc + ci, 0)
    state = lambda bi, ci: (bi, 0, 0, 0)
    return pl.pallas_call(
        _ret_scan_kernel,
        grid=(b, nc),
        in_specs=[
            pl.BlockSpec((chunk, dqk), tok),
            pl.BlockSpec((chunk, dqk), tok),
            pl.BlockSpec((chunk, dvv), tok),
            pl.BlockSpec((chunk, dvv), tok),
            pl.BlockSpec((1, H_RET, dk_w, dv_w), state),
            _resident((H_RET, chunk, chunk)),
            _resident((H_RET, chunk, dv_w)),
            _resident((H_RET, chunk, dk_w)),
            _resident((H_RET, 1, dv_w)),
        ],
        out_specs=[
            pl.BlockSpec((chunk, dvv), tok),
            pl.BlockSpec((1, H_RET, dk_w, dv_w), state),
        ],
        out_shape=[
            jax.ShapeDtypeStruct((t, dvv), BF16),
            jax.ShapeDtypeStruct((b, H_RET, dk_w, dv_w), F32),
        ],
        scratch_shapes=[pltpu.VMEM((H_RET, dk_w, dv_w), F32)],
        compiler_params=_params("parallel", "arbitrary"),
        name="ret_scan",
    )(q, k, v, gate, s0, din, dq, dk, dc)


def _sb_proj_kernel(x_ref, g_ref, w_ref, k_all_ref, v_all_ref, q_ref, kf_ref, vf_ref, kb_ref, vb_ref, *,
                    scale, keys_on_lanes):
    del k_all_ref, v_all_ref
    d = x_ref.shape[1]
    xn = _rmsnorm(x_ref[...], g_ref[...]).astype(BF16)
    q_ref[...] = (_dot(xn, w_ref[:, 0:d]) * scale).astype(BF16)
    k = _dot(xn, w_ref[:, d:2 * d])
    kb_ref[...] = k.astype(BF16)
    v = _dot(xn, w_ref[:, 2 * d:3 * d])
    vb_ref[...] = v.astype(BF16)
    if keys_on_lanes:
        kf_ref[0, 0] = k.T
        vf_ref[0, 0] = v.T
    else:
        kf_ref[0] = k
        vf_ref[0] = v


def _sb_proj(x, g, w_qkv, k_all, v_all, layer, seq_len=None):
    t, d = x.shape
    tm = min(TOKEN_TILE, t)
    assert t % tm == 0 and w_qkv.shape[1] == 3 * d
    scale = (d // H_SB) ** -0.5
    tok = pl.BlockSpec((tm, d), lambda i: (i, 0))
    if seq_len is None:
        assert k_all.shape[1:] == (t, d)
        slab = pl.BlockSpec((1, tm, d), lambda i: (layer, i, 0))
    else:
        tiles = seq_len // tm
        assert tiles * tm == seq_len and k_all.shape[1:] == (t // seq_len, d, seq_len)
        slab = pl.BlockSpec((1, 1, d, tm), lambda i: (layer, i // tiles, 0, i % tiles))
    hbm = pl.BlockSpec(memory_space=pl.ANY)
    return pl.pallas_call(
        functools.partial(_sb_proj_kernel, scale=scale, keys_on_lanes=seq_len is not None),
        grid=(t // tm,),
        in_specs=[
            tok,
            _resident((1, d)),
            _resident((d, 3 * d)),
            hbm,
            hbm,
        ],
        out_specs=[tok, slab, slab, tok, tok],
        out_shape=[
            jax.ShapeDtypeStruct((t, d), BF16),
            jax.ShapeDtypeStruct(k_all.shape, F32),
            jax.ShapeDtypeStruct(v_all.shape, F32),
            jax.ShapeDtypeStruct((t, d), BF16),
            jax.ShapeDtypeStruct((t, d), BF16),
        ],
        input_output_aliases={3: 1, 4: 2},
        compiler_params=_params("parallel"),
        name="sb_proj",
    )(x, g, w_qkv, k_all, v_all)


def _softplus(z):
    return jnp.maximum(z, 0.0) + jnp.log(1.0 + jnp.exp(-jnp.abs(z)))


def _sb_chains(chains, carry, acc):
    carry, acc = list(carry), list(acc)
    zs = [_dot(q, k) if keys_on_lanes else _dot_nt(q, k) for _, q, k, _, _, _, keys_on_lanes in chains]
    sps = []
    for z, (_, _, _, _, _, mask, _) in zip(zs, chains):
        sp = _softplus(z)
        sps.append(sp if mask is None else jnp.where(mask, sp, 0.0))
    newers = []
    for sp, (_, _, _, _, tri, _, _) in zip(sps, chains):
        hi = sp.astype(BF16)
        lo = (sp - hi.astype(F32)).astype(BF16)
        newers.append(_dot(hi, tri) + _dot(lo, tri))
    weights = []
    for z, sp, newer, (h, _, _, _, _, mask, _) in zip(zs, sps, newers, chains):
        a = jnp.exp(z - sp - newer - carry[h])
        weights.append((a if mask is None else jnp.where(mask, a, 0.0)).astype(BF16))
        carry[h] = carry[h] + jnp.sum(sp, axis=-1, keepdims=True)
    for a, (h, _, _, v, _, _, keys_on_lanes) in zip(weights, chains):
        acc[h] = acc[h] + (_dot_nt(a, v) if keys_on_lanes else _dot(a, v))
    return carry, acc


def _min_mass(carry):
    m = jnp.min(carry[0])
    for c in carry[1:]:
        m = jnp.minimum(m, jnp.min(c))
    return m


def _sb_older_blocks(chains_of, first_kb, carry, acc):
    n = len(carry)

    def cond(state):
        return jnp.logical_and(state[0] >= 0, state[1] < SB_SKIP_MASS)

    def body(state):
        kb = state[0]
        c, a = _sb_chains(chains_of(kb), state[2:2 + n], state[2 + n:])
        return (kb - 1, _min_mass(c), *c, *a)

    state = lax.while_loop(cond, body, (first_kb, _min_mass(carry), *carry, *acc))
    return state[2 + n:]


def _strict_tri(n):
    r = lax.broadcasted_iota(jnp.int32, (n, n), 0)
    c = lax.broadcasted_iota(jnp.int32, (n, n), 1)
    return r, c


def _sb_attn_kernel(q_ref, k_ref, v_ref, tri_ref, o_ref, *, hd):
    qi = pl.program_id(2)
    tq = q_ref.shape[0]
    heads = tuple(range(q_ref.shape[1] // hd))
    lanes = [slice(h * hd, (h + 1) * hd) for h in heads]
    tri = tri_ref[...]
    row, col = _strict_tri(tq)
    diag_mask = col < row
    qs = [q_ref[:, lanes[h]] for h in heads]

    def chains_of(kb, mask=None):
        ks = pl.multiple_of(kb * tq, tq)
        return [(h, qs[h], k_ref[pl.ds(ks, tq), lanes[h]], v_ref[pl.ds(ks, tq), lanes[h]], tri, mask, False)
                for h in heads]

    carry0 = [jnp.zeros((tq, 1), F32) for _ in heads]
    acc0 = [jnp.zeros((tq, hd), F32) for _ in heads]

    def store(acc):
        for h in heads:
            o_ref[:, lanes[h]] = acc[h].astype(BF16)

    @pl.when(qi == 0)
    def _():
        store(_sb_chains(chains_of(0, diag_mask), carry0, acc0)[1])

    @pl.when(qi > 0)
    def _():
        carry, acc = _sb_chains(chains_of(qi, diag_mask) + chains_of(qi - 1), carry0, acc0)
        store(_sb_older_blocks(chains_of, qi - 2, carry, acc))


def _tri_matrix(n):
    r, c = _strict_tri(n)
    return (r > c).astype(BF16)


def _sb_attn(q, k, v, seq_len):
    t, d = q.shape
    b = t // seq_len
    tq = min(SB_BLOCK, seq_len)
    nq = seq_len // tq
    hp = min(SB_PROMPT_LANES, d)
    assert nq * tq == seq_len and d % hp == 0 and hp % LANES == 0
    qmap = lambda bi, hi, qi: (bi * nq + qi, hi)
    kvmap = lambda bi, hi, qi: (bi, hi)
    return pl.pallas_call(
        functools.partial(_sb_attn_kernel, hd=d // H_SB),
        grid=(b, d // hp, nq),
        in_specs=[
            pl.BlockSpec((tq, hp), qmap),
            pl.BlockSpec((seq_len, hp), kvmap),
            pl.BlockSpec((seq_len, hp), kvmap),
            _resident((tq, tq)),
        ],
        out_specs=pl.BlockSpec((tq, hp), qmap),
        out_shape=jax.ShapeDtypeStruct((t, d), BF16),
        compiler_params=_params("parallel", "parallel", "arbitrary"),
        name="sb_attn",
    )(q, k, v, _tri_matrix(tq))


def _sb_attn_sample_kernel(q_ref, kc_ref, vc_ref, kn_ref, vn_ref, tri_ref, trin_ref, o_ref, *, tk, hd):
    ls = q_ref.shape[0]
    past = kc_ref.shape[3]
    heads = tuple(range(q_ref.shape[1] // hd))
    lanes = [slice(h * hd, (h + 1) * hd) for h in heads]
    last_kb = past // tk - 1
    tri = tri_ref[...]
    trin = trin_ref[...]
    row, col = _strict_tri(ls)
    new_mask = col < row
    qs = [q_ref[:, lanes[h]] for h in heads]

    def cache_chains(kb):
        keys = pl.ds(pl.multiple_of(kb * tk, tk), tk)
        return [(h, qs[h], kc_ref[0, 0, lanes[h], keys].astype(BF16), vc_ref[0, 0, lanes[h], keys].astype(BF16),
                 tri, None, True) for h in heads]

    new_chains = [(h, qs[h], kn_ref[:, lanes[h]], vn_ref[:, lanes[h]], trin, new_mask, False) for h in heads]
    carry = [jnp.zeros((ls, 1), F32) for _ in heads]
    acc = [jnp.zeros((ls, hd), F32) for _ in heads]
    carry, acc = _sb_chains(new_chains + cache_chains(last_kb), carry, acc)
    acc = _sb_older_blocks(cache_chains, last_kb - 1, carry, acc)
    for h in heads:
        o_ref[:, lanes[h]] = acc[h].astype(BF16)


def _sb_attn_sample(q, k_cache, v_cache, layer, k_new, v_new, ls):
    t, d = q.shape
    b = t // ls
    _, cb, cd, past = k_cache.shape
    tk = min(SB_BLOCK, past)
    hp = min(SB_SAMPLE_LANES, d)
    assert cb == b and cd == d and past % tk == 0 and tk % LANES == 0 and d % hp == 0 and hp % LANES == 0
    new = pl.BlockSpec((ls, hp), lambda bi, hi: (bi, hi))
    old = pl.BlockSpec((1, 1, hp, past), lambda bi, hi: (layer, bi, hi, 0))
    return pl.pallas_call(
        functools.partial(_sb_attn_sample_kernel, tk=tk, hd=d // H_SB),
        grid=(b, d // hp),
        in_specs=[
            new, old, old, new, new,
            _resident((tk, tk)),
            _resident((ls, ls)),
        ],
        out_specs=new,
        out_shape=jax.ShapeDtypeStruct((t, d), BF16),
        compiler_params=_params("parallel", "parallel"),
        name="sb_attn_sample",
    )(q, k_cache, v_cache, k_new, v_new, _tri_matrix(tk), _tri_matrix(ls))


def _rotary_tables(pos, half, reps):
    freqs = ROPE_BASE ** (-jnp.arange(half, dtype=F32) / half)
    ang = pos.astype(F32)[:, None] * freqs[None, :]
    cos, sin = jnp.cos(ang), jnp.sin(ang)
    cos = jnp.concatenate([cos, cos], axis=-1)
    sin = jnp.concatenate([-sin, sin], axis=-1)
    return jnp.tile(cos, (reps, 1)), jnp.tile(sin, (reps, 1))


def kernel(x_prompt, x_sample, p_prompt, p_sample, state_ret, cache_sb_k, cache_sb_v, norm_ffn_a, w_ffn_a_gu, w_ffn_a_down, norm_mix, ret_w_in, ret_w_out, sb_w_qkv, sb_w_out, norm_ffn_b, w_ffn_b_gu, w_ffn_b_down, norm_ple, ple_w_gate, ple_w_proj, norm_final):
    bp, sp, d = x_prompt.shape
    bs, ls, _ = x_sample.shape
    depth = norm_ffn_a.shape[0]
    past = cache_sb_k.shape[2]
    dk_w = d // H_RET
    dv_w = ret_w_out.shape[1] // H_RET
    hd = d // H_SB

    hp = x_prompt.reshape(bp * sp, d)
    hs = x_sample.reshape(bs * ls, d)
    pp = p_prompt.reshape(depth, bp * sp, -1)
    ps = p_sample.reshape(depth, bs * ls, -1)
    row = lambda g: g.reshape(1, d)

    cos_p, sin_p = _rotary_tables(jnp.arange(sp), dk_w // 2, 1)
    cos_s, sin_s = _rotary_tables(past + jnp.arange(ls), dk_w // 2, bs)
    ret_chunk_p = min(RET_BLOCK, sp)
    ret_chunk_s = min(RET_CHUNK, ls)
    zero_state = jnp.zeros((bp, H_RET, dk_w, dv_w), F32)
    g_final = row(norm_final)

    n_sb = depth // 2
    kp_all = jnp.zeros((n_sb, bp, d, sp), F32)
    vp_all = jnp.zeros((n_sb, bp, d, sp), F32)
    ks_all = jnp.zeros((n_sb, bs * ls, d), F32)
    vs_all = jnp.zeros((n_sb, bs * ls, d), F32)
    keys_on_lanes = lambda c: jnp.transpose(c, (0, 1, 3, 4, 2)).reshape(c.shape[0], c.shape[1], d, c.shape[2])
    positions_first = lambda c: jnp.transpose(c.reshape(n_sb, bp, H_SB, hd, sp), (0, 1, 4, 2, 3))
    cache_k, cache_v = keys_on_lanes(cache_sb_k), keys_on_lanes(cache_sb_v)

    ret_p, ret_s = [], []
    for i in range(depth):
        j = i // 2
        wa_gu, wa_dn = w_ffn_a_gu[i].astype(BF16), w_ffn_a_down[i].astype(BF16)
        hp = _ffn(hp, row(norm_ffn_a[i]), wa_gu, wa_dn)
        hs = _ffn(hs, row(norm_ffn_a[i]), wa_gu, wa_dn)
        g_mix = row(norm_mix[i])
        if i % 2 == 0:
            w_in, w_out = ret_w_in[j].astype(BF16), ret_w_out[j].astype(BF16)
            q, k, v, gate = _ret_proj(hp, g_mix, w_in, cos_p, sin_p)
            mix_p, s_new = _ret_scan(q, k, v, gate, zero_state, sp, ret_chunk_p)
            ret_p.append(s_new)
            q, k, v, gate = _ret_proj(hs, g_mix, w_in, cos_s, sin_s)
            mix_s, s_new = _ret_scan(q, k, v, gate, state_ret[j], ls, ret_chunk_s)
            ret_s.append(s_new)
        else:
            w_qkv, w_out = sb_w_qkv[j].astype(BF16), sb_w_out[j].astype(BF16)
            q, kp_all, vp_all, kb, vb = _sb_proj(hp, g_mix, w_qkv, kp_all, vp_all, j, seq_len=sp)
            mix_p = _sb_attn(q, kb, vb, sp)
            q, ks_all, vs_all, kb, vb = _sb_proj(hs, g_mix, w_qkv, ks_all, vs_all, j)
            mix_s = _sb_attn_sample(q, cache_k, cache_v, j, kb, vb, ls)
        post = (w_out, row(norm_ffn_b[i]), w_ffn_b_gu[i].astype(BF16), w_ffn_b_down[i].astype(BF16),
                row(norm_ple[i]), ple_w_gate[i].astype(BF16), ple_w_proj[i].astype(BF16), g_final, i == depth - 1)
        hp = _post_mixer(mix_p, hp, pp, i, *post)
        hs = _post_mixer(mix_s, hs, ps, i, *post)

    return (hp.reshape(bp, sp, d), hs.reshape(bs, ls, d),
            jnp.stack(ret_p), jnp.stack(ret_s),
            positions_first(kp_all), positions_first(vp_all),
            ks_all.reshape(n_sb, bs, ls, H_SB, hd), vs_all.reshape(n_sb, bs, ls, H_SB, hd))
```
